```python
import jax, jax.numpy as jnp
from jax import lax
import numpy as np

D_MODEL = 1024
BATCH = 4
SEQ = 4096
DEPTH = 2
DEC_BATCH = 32
DEC_SEQ = 32
PAST_LEN = 4096

CHUNK = 64
D_MIX = D_MODEL
D_POOL = D_MIX // 2
N_POOL_GROUPS = 4
POOL_GROUP = D_POOL // N_POOL_GROUPS
POOL_WINDOWS = (2, 4, 8, 16)
POOL_HIST = max(POOL_WINDOWS) - 1
D_GATE = D_MIX - D_POOL
N_GMLP_HEADS = 4
GMLP_HEAD = D_GATE // N_GMLP_HEADS
GMLP_CHUNK = 128
D_IN = D_POOL + 2 * D_GATE
N_MEM = 256
N_MEM_HEADS = 4
MEM_HEAD_DIM = D_MODEL // N_MEM_HEADS
D_FF = 2816
N_EXPERTS = 8
TOP_K = 2
N_DENSE = (DEPTH + 1) // 2
N_MOE = DEPTH // 2
EPS = 1e-6

kernel_name = 'hybrid_pool_gmlp_stream_step'


def rmsnorm(x, g):
    xf = x.astype(jnp.float32)
    y = xf * lax.rsqrt(jnp.mean(xf * xf, axis=-1, keepdims=True) + EPS)
    return (y * g.astype(jnp.float32)).astype(x.dtype)


def layernorm(x, g, b):
    xf = x.astype(jnp.float32)
    mu = jnp.mean(xf, axis=-1, keepdims=True)
    xc = xf - mu
    var = jnp.mean(xc * xc, axis=-1, keepdims=True)
    return (xc * lax.rsqrt(var + EPS) * g.astype(jnp.float32) + b.astype(jnp.float32)).astype(x.dtype)


def multiscale_pool(p, hist, pos0):
    L = p.shape[1]
    full = jnp.concatenate([hist, p], axis=1)
    cs = jnp.cumsum(full.astype(jnp.float32), axis=1)
    cs = jnp.concatenate([jnp.zeros_like(cs[:, :1]), cs], axis=1)
    end = cs[:, POOL_HIST + 1:POOL_HIST + 1 + L]
    pos = pos0 + jnp.arange(L)
    means = []
    for g, w in enumerate(POOL_WINDOWS):
        sl = slice(g * POOL_GROUP, (g + 1) * POOL_GROUP)
        start = cs[:, POOL_HIST + 1 - w:POOL_HIST + 1 - w + L, sl]
        cnt = jnp.minimum(pos + 1, w).astype(jnp.float32)[None, :, None]
        means.append((end[..., sl] - start) / cnt)
    mean = jnp.concatenate(means, axis=-1)
    diff = (mean - p.astype(jnp.float32)).astype(p.dtype)
    return diff, full[:, -POOL_HIST:]


def spatial_gate(u, v, ws, bs):
    B, L, H, Dh = v.shape
    cl = min(L, GMLP_CHUNK)
    n = L // cl
    mask = jnp.tril(jnp.ones((cl, cl), dtype=bool))
    w = jnp.where(mask[None], ws[:, :cl, :cl], 0)
    vc = v.reshape(B, n, cl, H, Dh)
    mixed = jnp.einsum('hts,bnshd->bnthd', w, vc) + bs[:, :cl].T[None, None, :, :, None]
    return u * mixed.reshape(B, L, H, Dh)


def token_mixer(h, pool_hist, pos0, w_in, pool_w, pool_scale, gm_ln_g, gm_ln_b, gm_ws, gm_bs, w_out):
    B, L, _ = h.shape
    z = h @ w_in
    p = z[..., :D_POOL]
    u = jax.nn.gelu(z[..., D_POOL:D_POOL + D_GATE])
    v = layernorm(jax.nn.gelu(z[..., D_POOL + D_GATE:]), gm_ln_g, gm_ln_b)
    pd, new_hist = multiscale_pool(p, pool_hist, pos0)
    a = jnp.einsum('blgc,gcd->blgd', pd.reshape(B, L, N_POOL_GROUPS, POOL_GROUP), pool_w)
    a = a.reshape(B, L, D_POOL) * pool_scale
    gated = spatial_gate(u.reshape(B, L, N_GMLP_HEADS, GMLP_HEAD),
                         v.reshape(B, L, N_GMLP_HEADS, GMLP_HEAD), gm_ws, gm_bs).reshape(B, L, D_GATE)
    out = jnp.concatenate([a, gated], axis=-1) @ w_out
    return out, new_hist, v


def memory_kv(mem, g, wk, wv):
    B = mem.shape[0]
    m = rmsnorm(mem, g)
    k = (m @ wk).reshape(B, N_MEM, N_MEM_HEADS, MEM_HEAD_DIM)
    v = (m @ wv).reshape(B, N_MEM, N_MEM_HEADS, MEM_HEAD_DIM)
    return k, v


def cross_attention(h, k, v, wq, wo):
    B, L, _ = h.shape
    q = (h @ wq).reshape(B, L, N_MEM_HEADS, MEM_HEAD_DIM)
    s = jnp.einsum('blhd,bmhd->bhlm', q, k).astype(jnp.float32) * (MEM_HEAD_DIM ** -0.5)
    prob = jax.nn.softmax(s, axis=-1).astype(v.dtype)
    o = jnp.einsum('bhlm,bmhd->blhd', prob, v).reshape(B, L, D_MODEL)
    return o @ wo


def swiglu(h, wg, wu, wd):
    return (jax.nn.silu(h @ wg) * (h @ wu)) @ wd


def moe_swiglu(h, router_w, router_b, wg, wu, wd):
    logits = (h @ router_w).astype(jnp.float32) + router_b.astype(jnp.float32)
    top_val, top_idx = lax.top_k(logits, TOP_K)
    top_gate = jax.nn.softmax(top_val, axis=-1)
    gate = jnp.einsum('blk,blke->ble', top_gate,
                      jax.nn.one_hot(top_idx, N_EXPERTS, dtype=jnp.float32)).astype(h.dtype)
    out = jnp.zeros_like(h)
    for e in range(N_EXPERTS):
        out = out + gate[..., e:e + 1] * swiglu(h, wg[e], wu[e], wd[e])
    return out


def run_trunk(x, pos0, mem_k, mem_v, pool_hist, keep_v, p):
    hists, vrows = [], []
    for l in range(DEPTH):
        h = rmsnorm(x, p['norm_mix_g'][l])
        out, nh, vr = token_mixer(h, pool_hist[l], pos0, p['w_in'][l], p['pool_w'][l], p['pool_scale'][l],
                                  p['gm_ln_g'][l], p['gm_ln_b'][l], p['gm_ws'][l], p['gm_bs'][l],
                                  p['w_mix_out'][l])
        x = x + out
        h = rmsnorm(x, p['norm_xa_g'][l])
        x = x + cross_attention(h, mem_k[l], mem_v[l], p['xa_wq'][l], p['xa_wo'][l])
        h = rmsnorm(x, p['norm_ffn_g'][l])
        if l % 2 == 0:
            i = l // 2
            x = x + swiglu(h, p['ffn_wg'][i], p['ffn_wu'][i], p['ffn_wd'][i])
        else:
            i = l // 2
            x = x + moe_swiglu(h, p['moe_router_w'][i], p['moe_router_b'][i],
                               p['moe_wg'][i], p['moe_wu'][i], p['moe_wd'][i])
        hists.append(nh)
        if keep_v:
            vrows.append(vr)
    y = rmsnorm(x, p['norm_final_g'])
    return y, jnp.stack(hists), (jnp.stack(vrows) if keep_v else None)


def setup_inputs(seed: int = 0) -> dict:
    key = jax.random.key(seed)
    ks = iter(jax.random.split(key, 40))
    f32 = jnp.float32

    def nrm(shape, scale=1.0):
        return jax.random.normal(next(ks), shape, f32) * scale

    def gain(shape):
        return 1.0 + nrm(shape, 0.05)

    return {
        'x_prompt': nrm((BATCH, SEQ, D_MODEL)),
        'x_sample': nrm((DEC_BATCH, DEC_SEQ, D_MODEL)),
        'cache_mem_k': nrm((DEPTH, DEC_BATCH, N_MEM, N_MEM_HEADS, MEM_HEAD_DIM)),
        'cache_mem_v': nrm((DEPTH, DEC_BATCH, N_MEM, N_MEM_HEADS, MEM_HEAD_DIM)),
        'state_pool': nrm((DEPTH, DEC_BATCH, POOL_HIST, D_POOL)),
        'mem_prompt': nrm((BATCH, N_MEM, D_MODEL)),
        'norm_mix_g': gain((DEPTH, D_MODEL)),
        'w_in': nrm((DEPTH, D_MODEL, D_IN), D_MODEL ** -0.5),
        'pool_w': nrm((DEPTH, N_POOL_GROUPS, POOL_GROUP, POOL_GROUP), POOL_GROUP ** -0.5),
        'pool_scale': gain((DEPTH, D_POOL)),
        'gm_ln_g': gain((DEPTH, D_GATE)),
        'gm_ln_b': nrm((DEPTH, D_GATE), 0.02),
        'gm_ws': nrm((DEPTH, N_GMLP_HEADS, GMLP_CHUNK, GMLP_CHUNK), GMLP_CHUNK ** -0.5),
        'gm_bs': gain((DEPTH, N_GMLP_HEADS, GMLP_CHUNK)),
        'w_mix_out': nrm((DEPTH, D_MIX, D_MODEL), D_MIX ** -0.5),
        'norm_xa_g': gain((DEPTH, D_MODEL)),
        'norm_mem_g': gain((DEPTH, D_MODEL)),
        'xa_wq': nrm((DEPTH, D_MODEL, D_MODEL), D_MODEL ** -0.5),
        'xa_wk': nrm((DEPTH, D_MODEL, D_MODEL), D_MODEL ** -0.5),
        'xa_wv': nrm((DEPTH, D_MODEL, D_MODEL), D_MODEL ** -0.5),
        'xa_wo': nrm((DEPTH, D_MODEL, D_MODEL), D_MODEL ** -0.5),
        'norm_ffn_g': gain((DEPTH, D_MODEL)),
        'ffn_wg': nrm((N_DENSE, D_MODEL, D_FF), D_MODEL ** -0.5),
        'ffn_wu': nrm((N_DENSE, D_MODEL, D_FF), D_MODEL ** -0.5),
        'ffn_wd': nrm((N_DENSE, D_FF, D_MODEL), D_FF ** -0.5),
        'moe_router_w': nrm((N_MOE, D_MODEL, N_EXPERTS), D_MODEL ** -0.5),
        'moe_router_b': nrm((N_MOE, N_EXPERTS), 0.01),
        'moe_wg': nrm((N_MOE, N_EXPERTS, D_MODEL, D_FF), D_MODEL ** -0.5),
        'moe_wu': nrm((N_MOE, N_EXPERTS, D_MODEL, D_FF), D_MODEL ** -0.5),
        'moe_wd': nrm((N_MOE, N_EXPERTS, D_FF, D_MODEL), D_FF ** -0.5),
        'norm_final_g': gain((D_MODEL,)),
    }


def reference(x_prompt, x_sample, cache_mem_k, cache_mem_v, state_pool, mem_prompt,
              norm_mix_g, w_in, pool_w, pool_scale, gm_ln_g, gm_ln_b, gm_ws, gm_bs, w_mix_out,
              norm_xa_g, norm_mem_g, xa_wq, xa_wk, xa_wv, xa_wo,
              norm_ffn_g, ffn_wg, ffn_wu, ffn_wd,
              moe_router_w, moe_router_b, moe_wg, moe_wu, moe_wd, norm_final_g):
    params = dict(norm_mix_g=norm_mix_g, w_in=w_in, pool_w=pool_w, pool_scale=pool_scale,
                  gm_ln_g=gm_ln_g, gm_ln_b=gm_ln_b, gm_ws=gm_ws, gm_bs=gm_bs, w_mix_out=w_mix_out,
                  norm_xa_g=norm_xa_g, xa_wq=xa_wq, xa_wo=xa_wo, norm_ffn_g=norm_ffn_g,
                  ffn_wg=ffn_wg, ffn_wu=ffn_wu, ffn_wd=ffn_wd,
                  moe_router_w=moe_router_w, moe_router_b=moe_router_b,
                  moe_wg=moe_wg, moe_wu=moe_wu, moe_wd=moe_wd, norm_final_g=norm_final_g)

    kv = [memory_kv(mem_prompt, norm_mem_g[l], xa_wk[l], xa_wv[l]) for l in range(DEPTH)]
    new_mem_k_prompt = jnp.stack([k for k, _ in kv])
    new_mem_v_prompt = jnp.stack([v for _, v in kv])
    empty_hist = jnp.zeros((DEPTH, BATCH, POOL_HIST, D_POOL), dtype=x_prompt.dtype)
    y_prompt, new_pool_prompt, _ = run_trunk(x_prompt, 0, new_mem_k_prompt, new_mem_v_prompt,
                                             empty_hist, False, params)

    y_sample, new_pool_sample, new_gmlp_v_sample = run_trunk(x_sample, PAST_LEN, cache_mem_k, cache_mem_v,
                                                             state_pool, True, params)

    return (y_prompt, y_sample, new_mem_k_prompt, new_mem_v_prompt,
            new_pool_prompt, new_pool_sample, new_gmlp_v_sample)
```

```python
import functools

import jax
import jax.numpy as jnp
from jax import lax
from jax.experimental import pallas as pl
from jax.experimental.pallas import tpu as pltpu

EPS = 1e-6
PAST_LEN = 4096
POOL_WINDOWS = (2, 4, 8, 16)
HIST_ROWS = 16
GMLP_CHUNK = 128
LANES = 128
TOP_K = 2

V7X_VMEM_BYTES = 64 * 1024 * 1024
VMEM_LIMIT = V7X_VMEM_BYTES - 8 * 1024 * 1024

MIX_ROWS = 256
ATTN_ROWS = 512
FFN_ROWS = 512
FF_CHUNK = 256
MOVE_ROWS = 256

BF16 = jnp.bfloat16
F32 = jnp.float32


def _rmsnorm(x, g):
    return x * lax.rsqrt(jnp.mean(x * x, axis=-1, keepdims=True) + EPS) * g


def _dot(a, b):
    return jnp.dot(a, b, preferred_element_type=F32)


def _params(sem, vmem=VMEM_LIMIT):
    return pltpu.CompilerParams(dimension_semantics=sem, vmem_limit_bytes=vmem)


def _const_spec(shape, layer=None):
    nd = len(shape)
    if layer is None:
        return pl.BlockSpec(shape, lambda *_: (0,) * nd)
    return pl.BlockSpec((None,) + shape, lambda *_: (layer,) + (0,) * nd)


def _memkv_kernel(mem_ref, g_ref, wk_ref, wv_ref, k_ref, v_ref):
    m = _rmsnorm(mem_ref[...], g_ref[...]).astype(BF16)
    k_ref[...] = _dot(m, wk_ref[...])
    v_ref[...] = _dot(m, wv_ref[...])


def _memory_kv(mem2d, norm_g, wk, wv):
    depth, d = norm_g.shape
    rows = mem2d.shape[0]
    lay = lambda l: (l, 0, 0)
    return pl.pallas_call(
        _memkv_kernel,
        grid=(depth,),
        in_specs=[pl.BlockSpec((rows, d), lambda l: (0, 0)),
                  pl.BlockSpec((None, 1, d), lay),
                  pl.BlockSpec((None, d, d), lay),
                  pl.BlockSpec((None, d, d), lay)],
        out_specs=[pl.BlockSpec((None, rows, d), lay)] * 2,
        out_shape=[jax.ShapeDtypeStruct((depth, rows, d), F32)] * 2,
        compiler_params=_params(("arbitrary",)),
        name="memory_kv",
    )(mem2d, norm_g.reshape(depth, 1, d), wk, wv)


def _mixer_kernel(x_ref, g_ref, win_ref, poolw_ref, pscale_ref, lng_ref, lnb_ref, wbig_ref,
                  bias_ref, wout_ref, hist_ref, o_ref, hist_out_ref, *rest, n_seg, seg_rows, pos0,
                  keep_v):
    v_ref = rest[0] if keep_v else None
    hist_sc = rest[-1]
    j = pl.program_id(1)
    d_pool = pscale_ref.shape[-1]
    d_gate = lng_ref.shape[-1]
    pg = d_pool // len(POOL_WINDOWS)
    n_heads = wbig_ref.shape[0]
    gh = d_gate // n_heads

    @pl.when(j == 0)
    def _():
        hist_sc[...] = hist_ref[...]

    x = x_ref[...]
    h = _rmsnorm(x, g_ref[...]).astype(BF16)
    z = _dot(h, win_ref[...])
    p = z[:, :d_pool]
    u = jax.nn.gelu(z[:, d_pool:d_pool + d_gate])
    vpre = jax.nn.gelu(z[:, d_pool + d_gate:])
    mu = jnp.mean(vpre, axis=-1, keepdims=True)
    vc = vpre - mu
    var = jnp.mean(vc * vc, axis=-1, keepdims=True)
    v = vc * lax.rsqrt(var + EPS) * lng_ref[...] + lnb_ref[...]
    if keep_v:
        v_ref[...] = v

    pos = pos0 + j * seg_rows + lax.broadcasted_iota(jnp.int32, (seg_rows, pg), 0)
    pd_segs = []
    for s in range(n_seg):
        ps = p[s * seg_rows:(s + 1) * seg_rows]
        ext = jnp.concatenate([hist_sc[s], ps], axis=0)
        hist_sc[s] = ps[seg_rows - HIST_ROWS:]
        cols = []
        for gi, w in enumerate(POOL_WINDOWS):
            acc = ext[:, gi * pg:(gi + 1) * pg]
            span = 1
            while span < w:
                acc = acc + pltpu.roll(acc, span, 0)
                span *= 2
            cnt = jnp.minimum(pos + 1, w).astype(F32)
            cols.append(acc[HIST_ROWS:] / cnt - ps[:, gi * pg:(gi + 1) * pg])
        pd_segs.append(jnp.concatenate(cols, axis=1))
    pd = pd_segs[0] if n_seg == 1 else jnp.concatenate(pd_segs, axis=0)
    hist_out_ref[...] = hist_sc[...]

    pdb = pd.astype(BF16)
    parts = []
    for gi in range(len(POOL_WINDOWS)):
        sl = slice(gi * pg, (gi + 1) * pg)
        parts.append(_dot(pdb[:, sl], poolw_ref[gi]) * pscale_ref[:, sl])
    vb = v.astype(BF16)
    for hi in range(n_heads):
        sl = slice(hi * gh, (hi + 1) * gh)
        mixed = _dot(wbig_ref[hi], vb[:, sl]) + bias_ref[:, sl]
        parts.append(u[:, sl] * mixed)
    cat = jnp.concatenate(parts, axis=1).astype(BF16)
    o_ref[...] = x + _dot(cat, wout_ref[...])


def _mixer(x_all, off, n_b, n_j, n_seg, seg_rows, pos0, hist, layer, w, wbig, bias, keep_v):
    d = x_all.shape[1]
    rows = n_seg * seg_rows
    d_in = w["w_in"].shape[-1]
    d_pool = w["pool_scale"].shape[-1]
    d_gate = w["gm_ln_g"].shape[-1]
    vec = lambda a: a[layer].reshape(1, -1)
    xspec = pl.BlockSpec((rows, d), lambda b, j: (off + b * n_j + j, 0))
    hspec = pl.BlockSpec((n_seg, HIST_ROWS, d_pool), lambda b, j: (b, 0, 0))
    in_specs = [xspec, _const_spec((1, d)), _const_spec((d, d_in), layer),
                _const_spec(w["pool_w"].shape[1:], layer),
                _const_spec((1, d_pool)), _const_spec((1, d_gate)), _const_spec((1, d_gate)),
                _const_spec(wbig.shape), _const_spec(bias.shape), _const_spec((d, d), layer),
                hspec]
    out_specs = [xspec, hspec]
    out_shape = [jax.ShapeDtypeStruct(x_all.shape, F32), jax.ShapeDtypeStruct(hist.shape, F32)]
    if keep_v:
        out_specs.append(pl.BlockSpec((rows, d_gate), lambda b, j: (b * n_j + j, 0)))
        out_shape.append(jax.ShapeDtypeStruct((n_b * n_j * rows, d_gate), F32))
    return pl.pallas_call(
        functools.partial(_mixer_kernel, n_seg=n_seg, seg_rows=seg_rows, pos0=pos0, keep_v=keep_v),
        grid=(n_b, n_j),
        in_specs=in_specs,
        out_specs=out_specs,
        out_shape=out_shape,
        scratch_shapes=[pltpu.VMEM((n_seg, HIST_ROWS, d_pool), F32)],
        input_output_aliases={0: 0},
        compiler_params=_params(("arbitrary", "arbitrary")),
        name="token_mixer",
    )(x_all, vec(w["norm_mix_g"]), w["w_in"], w["pool_w"], vec(w["pool_scale"]),
      vec(w["gm_ln_g"]), vec(w["gm_ln_b"]), wbig, bias, w["w_mix_out"], hist)


def _gate_operands(ws, bs, rows, chunk):
    n_heads = ws.shape[0]
    gh = GMLP_CHUNK
    tri = jnp.tril(jnp.ones((chunk, chunk), dtype=bool))
    wc = jnp.where(tri[None], ws[:, :chunk, :chunk], 0)
    eye = jnp.eye(rows // chunk, dtype=ws.dtype)
    wbig = jnp.einsum("ab,hts->hatbs", eye, wc).reshape(n_heads, rows, rows).astype(BF16)
    bias = jnp.tile(bs[:, :chunk].T, (rows // chunk, 1))
    bias = jnp.repeat(bias, gh, axis=1)
    return wbig, bias


def _attn_kernel(x_ref, g_ref, wq_ref, wo_ref, k_ref, v_ref, o_ref, *, n_heads):
    x = x_ref[...]
    d = x.shape[-1]
    hd = d // n_heads
    h = _rmsnorm(x, g_ref[...]).astype(BF16)
    q = (_dot(h, wq_ref[...]) * (hd ** -0.5)).astype(BF16)
    kb = k_ref[...].astype(BF16)
    vb = v_ref[...].astype(BF16)
    outs = []
    for hi in range(n_heads):
        sl = slice(hi * hd, (hi + 1) * hd)
        s = lax.dot_general(q[:, sl], kb[:, sl], (((1,), (1,)), ((), ())),
                            preferred_element_type=F32)
        e = jnp.exp(s - jnp.max(s, axis=-1, keepdims=True))
        prob = e * (1.0 / jnp.sum(e, axis=-1, keepdims=True))
        outs.append(_dot(prob.astype(BF16), vb[:, sl]))
    o = jnp.concatenate(outs, axis=1).astype(BF16)
    o_ref[...] = x + _dot(o, wo_ref[...])


def _attention(x_all, off, n_b, n_j, rows, mem_k, mem_v, layer, kv_layer, w, n_heads):
    d = x_all.shape[1]
    n_mem = mem_k.shape[2]
    xmap = lambda b, j: (off + b * n_j + j, 0)
    kvspec = pl.BlockSpec((None, None, n_mem, d), lambda b, j: (kv_layer, b, 0, 0))
    return pl.pallas_call(
        functools.partial(_attn_kernel, n_heads=n_heads),
        grid=(n_b, n_j),
        in_specs=[pl.BlockSpec((rows, d), xmap), _const_spec((1, d)),
                  _const_spec((d, d), layer), _const_spec((d, d), layer), kvspec, kvspec],
        out_specs=pl.BlockSpec((rows, d), xmap),
        out_shape=jax.ShapeDtypeStruct(x_all.shape, F32),
        input_output_aliases={0: 0},
        compiler_params=_params(("arbitrary", "arbitrary")),
        name="cross_attention",
    )(x_all, w["norm_xa_g"][layer].reshape(1, d), w["xa_wq"], w["xa_wo"], mem_k, mem_v)


def _swiglu_into(o_ref, hb, wg_ref, wu_ref, wd_ref, base):
    d_ff = wg_ref.shape[-1]
    for c in range(d_ff // FF_CHUNK):
        sl = slice(c * FF_CHUNK, (c + 1) * FF_CHUNK)
        a = (jax.nn.silu(_dot(hb, wg_ref[:, sl])) * _dot(hb, wu_ref[:, sl])).astype(BF16)
        part = _dot(a, wd_ref[sl, :])
        if c == 0:
            o_ref[...] = part if base is None else base + part
        else:
            o_ref[...] += part


def _ffn_kernel(x_ref, g_ref, wg_ref, wu_ref, wd_ref, gf_ref, o_ref, *, final):
    x = x_ref[...]
    hb = _rmsnorm(x, g_ref[...]).astype(BF16)
    _swiglu_into(o_ref, hb, wg_ref, wu_ref, wd_ref, x)
    if final:
        o_ref[...] = _rmsnorm(o_ref[...], gf_ref[...])


def _dense_ffn(x_all, g, wg, wu, wd, idx, g_final, final):
    t, d = x_all.shape
    d_ff = wg.shape[-1]
    assert t % FFN_ROWS == 0 and d_ff % FF_CHUNK == 0
    xspec = pl.BlockSpec((FFN_ROWS, d), lambda i: (i, 0))
    return pl.pallas_call(
        functools.partial(_ffn_kernel, final=final),
        grid=(t // FFN_ROWS,),
        in_specs=[xspec, _const_spec((1, d)), _const_spec((d, d_ff), idx),
                  _const_spec((d, d_ff), idx), _const_spec((d_ff, d), idx), _const_spec((1, d))],
        out_specs=xspec,
        out_shape=jax.ShapeDtypeStruct((t, d), F32),
        input_output_aliases={0: 0},
        compiler_params=_params(("arbitrary",)),
        name="dense_ffn",
    )(x_all, g.reshape(1, d), wg, wu, wd, g_final.reshape(1, d))


def _router_kernel(x_ref, g_ref, rw_ref, rb_ref, sel_ref, gate_ref, cnt_ref, carry):
    i = pl.program_id(0)
    rows = x_ref.shape[0]

    @pl.when(i == 0)
    def _():
        carry[...] = jnp.zeros_like(carry)

    h = _rmsnorm(x_ref[...], g_ref[...])
    logits = jnp.dot(h, rw_ref[...], preferred_element_type=F32,
                     precision=lax.Precision.HIGHEST) + rb_ref[...]
    lane_i = lax.broadcasted_iota(jnp.int32, logits.shape, 1)
    lane = lane_i.astype(F32)
    m0 = jnp.max(logits, axis=-1, keepdims=True)
    e0 = jnp.min(jnp.where(logits == m0, lane, float(LANES)), axis=-1, keepdims=True)
    rest = jnp.where(lane == e0, -jnp.inf, logits)
    m1 = jnp.max(rest, axis=-1, keepdims=True)
    e1 = jnp.min(jnp.where(rest == m1, lane, float(LANES)), axis=-1, keepdims=True)
    t = jnp.exp(m1 - m0)
    g0 = 1.0 / (1.0 + t)
    g1 = t * g0
    hot0 = (lane == e0).astype(F32)
    hot1 = (lane == e1).astype(F32)
    both = (hot0 + hot1).astype(BF16)
    r_i = lax.broadcasted_iota(jnp.int32, (rows, rows), 0)
    c_i = lax.broadcasted_iota(jnp.int32, (rows, rows), 1)
    before = (c_i < r_i).astype(BF16)
    prior = _dot(before, both) + carry[...]
    rank0 = jnp.sum(hot0 * prior, axis=-1, keepdims=True).astype(jnp.int32)
    rank1 = jnp.sum(hot1 * prior, axis=-1, keepdims=True).astype(jnp.int32)
    carry[...] += jnp.sum(hot0 + hot1, axis=0, keepdims=True)
    cnt_ref[...] = carry[...]
    e0i, e1i = e0.astype(jnp.int32), e1.astype(jnp.int32)
    sel_ref[...] = jnp.where(lane_i == 0, e0i, jnp.where(lane_i == 1, e1i,
                             jnp.where(lane_i == 2, rank0, jnp.where(lane_i == 3, rank1, 0))))
    gate_ref[...] = jnp.where(lane_i == 0, g0, jnp.where(lane_i == 1, g1, 0.0))


def _router(x_all, g, rw_pad, rb_pad):
    t, d = x_all.shape
    rows = FFN_ROWS
    ospec = pl.BlockSpec((rows, LANES), lambda i: (i, 0))
    return pl.pallas_call(
        _router_kernel,
        grid=(t // rows,),
        in_specs=[pl.BlockSpec((rows, d), lambda i: (i, 0)), _const_spec((1, d)),
                  _const_spec((d, LANES)), _const_spec((1, LANES))],
        out_specs=[ospec, ospec, _const_spec((1, LANES))],
        out_shape=[jax.ShapeDtypeStruct((t, LANES), jnp.int32),
                   jax.ShapeDtypeStruct((t, LANES), F32),
                   jax.ShapeDtypeStruct((1, LANES), F32)],
        scratch_shapes=[pltpu.VMEM((1, LANES), F32)],
        compiler_params=_params(("arbitrary",)),
        name="moe_router",
    )(x_all, g.reshape(1, d), rw_pad, rb_pad)


def _dispatch_kernel(meta_ref, x_ref, p0_ref, p1_ref, xs_ref, zero_sc, sem, zsem, *, n_exp,
                     min_tiles):
    i = pl.program_id(0)
    rows = x_ref.shape[0]
    tile = zero_sc.shape[0]
    n_tiles = xs_ref.shape[0] // tile

    def zero_copy(row_end):
        start = pl.multiple_of(row_end - tile, tile)
        return pltpu.make_async_copy(zero_sc, xs_ref.at[pl.ds(start, tile)], zsem)

    fills = [(meta_ref[n_exp + e] > 0, meta_ref[e]) for e in range(n_exp)]
    fills += [(k >= meta_ref[2 * n_exp], (k + 1) * tile) for k in range(min_tiles, n_tiles)]

    @pl.when(i == 0)
    def _():
        zero_sc[...] = jnp.zeros_like(zero_sc)
        for needed, row_end in fills:
            @pl.when(needed)
            def _():
                zero_copy(row_end).start()
        for needed, row_end in fills:
            @pl.when(needed)
            def _():
                zero_copy(row_end).wait()

    def start(r, c):
        src = x_ref.at[pl.ds(r, 1)]
        pltpu.make_async_copy(src, xs_ref.at[pl.ds(p0_ref[r], 1)], sem).start()
        pltpu.make_async_copy(src, xs_ref.at[pl.ds(p1_ref[r], 1)], sem).start()
        return c

    lax.fori_loop(0, rows, start, 0)
    for _ in range(TOP_K):
        pltpu.make_async_copy(x_ref, xs_ref.at[pl.ds(0, rows)], sem).wait()


def _dispatch(x_all, pos0, pos1, meta, n_exp, n_sorted, tile):
    t, d = x_all.shape
    rows = MOVE_ROWS
    smem_rows = pl.BlockSpec((rows,), lambda i, *_: (i,), memory_space=pltpu.SMEM)
    min_tiles = -(-TOP_K * t // tile)
    return pl.pallas_call(
        functools.partial(_dispatch_kernel, n_exp=n_exp, min_tiles=min_tiles),
        grid_spec=pltpu.PrefetchScalarGridSpec(
            num_scalar_prefetch=1,
            grid=(t // rows,),
            in_specs=[pl.BlockSpec((rows, d), lambda i, *_: (i, 0)), smem_rows, smem_rows],
            out_specs=pl.BlockSpec(memory_space=pl.ANY),
            scratch_shapes=[pltpu.VMEM((tile, d), F32), pltpu.SemaphoreType.DMA,
                            pltpu.SemaphoreType.DMA]),
        out_shape=jax.ShapeDtypeStruct((n_sorted, d), F32),
        compiler_params=_params(("arbitrary",)),
        name="moe_dispatch",
    )(meta, x_all, pos0, pos1)


def _expert_kernel(te_ref, nu_ref, xs_ref, g_ref, wg_ref, wu_ref, wd_ref, ys_ref):
    used = pl.program_id(0) < nu_ref[0]

    @pl.when(used)
    def _():
        hb = _rmsnorm(xs_ref[...], g_ref[...]).astype(BF16)
        _swiglu_into(ys_ref, hb, wg_ref, wu_ref, wd_ref, None)

    @pl.when(jnp.logical_not(used))
    def _():
        ys_ref[...] = jnp.zeros_like(ys_ref)


def _expert_ffn(xs, g, tile_expert, n_used, wg, wu, wd, idx, tile):
    n_sorted, d = xs.shape
    d_ff = wg.shape[-1]
    rowmap = lambda i, te, nu: (jnp.minimum(i, nu[0] - 1), 0)
    wmap = lambda i, te, nu: (idx, te[i], 0, 0)
    return pl.pallas_call(
        _expert_kernel,
        grid_spec=pltpu.PrefetchScalarGridSpec(
            num_scalar_prefetch=2,
            grid=(n_sorted // tile,),
            in_specs=[pl.BlockSpec((tile, d), rowmap),
                      pl.BlockSpec((1, d), lambda i, te, nu: (0, 0)),
                      pl.BlockSpec((None, None, d, d_ff), wmap),
                      pl.BlockSpec((None, None, d, d_ff), wmap),
                      pl.BlockSpec((None, None, d_ff, d), wmap)],
            out_specs=pl.BlockSpec((tile, d), lambda i, te, nu: (i, 0))),
        out_shape=jax.ShapeDtypeStruct((n_sorted, d), F32),
        compiler_params=_params(("arbitrary",)),
        name="expert_ffn",
    )(tile_expert, n_used, xs, g.reshape(1, d), wg, wu, wd)


def _combine_kernel(x_ref, p0_ref, p1_ref, gate_ref, gf_ref, ys_ref, o_ref, buf0, buf1, sem,
                    *, final):
    rows = x_ref.shape[0]

    def start(r, c):
        pltpu.make_async_copy(ys_ref.at[pl.ds(p0_ref[r], 1)], buf0.at[pl.ds(r, 1)], sem).start()
        pltpu.make_async_copy(ys_ref.at[pl.ds(p1_ref[r], 1)], buf1.at[pl.ds(r, 1)], sem).start()
        return c

    lax.fori_loop(0, rows, start, 0)
    for buf in (buf0, buf1):
        pltpu.make_async_copy(ys_ref.at[pl.ds(0, rows)], buf, sem).wait()
    gates = gate_ref[...]
    out = x_ref[...] + gates[:, 0:1] * buf0[...] + gates[:, 1:2] * buf1[...]
    if final:
        out = _rmsnorm(out, gf_ref[...])
    o_ref[...] = out


def _combine(x_all, off_rows, n_rows, pos0, pos1, gates, ys, g_final, final):
    d = x_all.shape[1]
    rows = MOVE_ROWS
    off = off_rows // rows
    smem_rows = pl.BlockSpec((rows,), lambda i: (off + i,), memory_space=pltpu.SMEM)
    return pl.pallas_call(
        functools.partial(_combine_kernel, final=final),
        grid=(n_rows // rows,),
        in_specs=[pl.BlockSpec((rows, d), lambda i: (off + i, 0)), smem_rows, smem_rows,
                  pl.BlockSpec((rows, LANES), lambda i: (off + i, 0)), _const_spec((1, d)),
                  pl.BlockSpec(memory_space=pl.ANY)],
        out_specs=pl.BlockSpec((rows, d), lambda i: (i, 0)),
        out_shape=jax.ShapeDtypeStruct((n_rows, d), F32),
        scratch_shapes=[pltpu.VMEM((rows, d), F32), pltpu.VMEM((rows, d), F32),
                        pltpu.SemaphoreType.DMA],
        compiler_params=_params(("arbitrary",)),
        name="moe_combine",
    )(x_all, pos0, pos1, gates, g_final.reshape(1, d), ys)


def _moe_layer(x_all, splits, g, rw, rb, wg, wu, wd, idx, g_final, final):
    t, d = x_all.shape
    n_exp = rw.shape[-1]
    tile = FFN_ROWS
    rw_pad = jnp.zeros((d, LANES), F32).at[:, :n_exp].set(rw)
    rb_pad = jnp.full((1, LANES), -jnp.inf, F32).at[0, :n_exp].set(rb)
    sel, gates, counts = _router(x_all, g, rw_pad, rb_pad)

    counts = counts[0, :n_exp].astype(jnp.int32)
    gpad = (counts + tile - 1) // tile * tile
    gend = jnp.cumsum(gpad)
    gstart = gend - gpad
    pos0 = gstart[sel[:, 0]] + sel[:, 2]
    pos1 = gstart[sel[:, 1]] + sel[:, 3]
    n_tiles = (TOP_K * t + n_exp * (tile - 1)) // tile
    tile_ids = jnp.arange(n_tiles, dtype=jnp.int32)
    tile_expert = jnp.minimum(
        jnp.sum((gend // tile)[None, :] <= tile_ids[:, None], axis=1), n_exp - 1).astype(jnp.int32)
    n_used = (gend[-1:] // tile).astype(jnp.int32)

    meta = jnp.concatenate([gend, gpad, n_used]).astype(jnp.int32)
    xs = _dispatch(x_all, pos0, pos1, meta, n_exp, n_tiles * tile, tile)
    ys = _expert_ffn(xs, g, tile_expert, n_used, wg, wu, wd, idx, tile)
    return [_combine(x_all, off, n, pos0, pos1, gates, ys, g_final, final) for off, n in splits]


def kernel(x_prompt, x_sample, cache_mem_k, cache_mem_v, state_pool, mem_prompt, norm_mix_g, w_in, pool_w, pool_scale, gm_ln_g, gm_ln_b, gm_ws, gm_bs, w_mix_out, norm_xa_g, norm_mem_g, xa_wq, xa_wk, xa_wv, xa_wo, norm_ffn_g, ffn_wg, ffn_wu, ffn_wd, moe_router_w, moe_router_b, moe_wg, moe_wu, moe_wd, norm_final_g):
    batch, seq, d = x_prompt.shape
    dec_batch, dec_seq, _ = x_sample.shape
    depth = norm_mix_g.shape[0]
    n_mem, n_mem_heads, mem_hd = cache_mem_k.shape[2:]
    d_pool = pool_scale.shape[-1]
    d_gate = gm_ln_g.shape[-1]
    rows_p, rows_s = batch * seq, dec_batch * dec_seq
    total = rows_p + rows_s
    seg_per_tile = MIX_ROWS // dec_seq
    assert seq % MIX_ROWS == 0 and seq % ATTN_ROWS == 0 and MIX_ROWS % GMLP_CHUNK == 0
    assert MIX_ROWS % dec_seq == 0 and dec_batch % seg_per_tile == 0 and dec_seq >= HIST_ROWS
    assert dec_seq <= GMLP_CHUNK and PAST_LEN % GMLP_CHUNK == 0
    assert total % FFN_ROWS == 0 and rows_p % MOVE_ROWS == 0 and rows_s % MOVE_ROWS == 0

    w = dict(norm_mix_g=norm_mix_g, w_in=w_in.astype(BF16), pool_w=pool_w.astype(BF16),
             pool_scale=pool_scale, gm_ln_g=gm_ln_g, gm_ln_b=gm_ln_b,
             w_mix_out=w_mix_out.astype(BF16), norm_xa_g=norm_xa_g,
             xa_wq=xa_wq.astype(BF16), xa_wo=xa_wo.astype(BF16))
    ffn_b = [a.astype(BF16) for a in (ffn_wg, ffn_wu, ffn_wd)]
    moe_b = [a.astype(BF16) for a in (moe_wg, moe_wu, moe_wd)]

    mk, mv = _memory_kv(mem_prompt.reshape(batch * n_mem, d), norm_mem_g,
                        xa_wk.astype(BF16), xa_wv.astype(BF16))
    new_mem_k = mk.reshape(depth, batch, n_mem, n_mem_heads, mem_hd)
    new_mem_v = mv.reshape(depth, batch, n_mem, n_mem_heads, mem_hd)
    pk, pv = mk.reshape(depth, batch, n_mem, d), mv.reshape(depth, batch, n_mem, d)
    ck = cache_mem_k.reshape(depth, dec_batch, n_mem, d)
    cv = cache_mem_v.reshape(depth, dec_batch, n_mem, d)

    hist_p0 = jnp.zeros((batch, HIST_ROWS, d_pool), F32)
    hist_s = jnp.pad(state_pool, ((0, 0), (0, 0), (HIST_ROWS - state_pool.shape[2], 0), (0, 0)))
    keep = HIST_ROWS - state_pool.shape[2]

    x_all = jnp.concatenate([x_prompt.reshape(rows_p, d), x_sample.reshape(rows_s, d)], axis=0)
    pool_p, pool_s, v_rows = [], [], []
    y_p = y_s = None
    for l in range(depth):
        wbig_p, bias_p = _gate_operands(gm_ws[l], gm_bs[l], MIX_ROWS, GMLP_CHUNK)
        wbig_s, bias_s = _gate_operands(gm_ws[l], gm_bs[l], MIX_ROWS, dec_seq)
        x_all, hp = _mixer(x_all, 0, batch, seq // MIX_ROWS, 1, MIX_ROWS, 0, hist_p0, l, w,
                           wbig_p, bias_p, False)
        x_all, hs, vr = _mixer(x_all, rows_p // MIX_ROWS, dec_batch // seg_per_tile, 1,
                               seg_per_tile, dec_seq, PAST_LEN, hist_s[l], l, w, wbig_s, bias_s,
                               True)
        pool_p.append(hp[:, keep:])
        pool_s.append(hs[:, keep:])
        v_rows.append(vr.reshape(dec_batch, dec_seq, d_gate))

        x_all = _attention(x_all, 0, batch, seq // ATTN_ROWS, ATTN_ROWS, pk, pv, l, l, w,
                           n_mem_heads)
        x_all = _attention(x_all, rows_p // dec_seq, dec_batch, 1, dec_seq, ck, cv, l, l, w,
                           n_mem_heads)

        final = l == depth - 1
        if l % 2 == 0:
            x_all = _dense_ffn(x_all, norm_ffn_g[l], *ffn_b, l // 2, norm_final_g, final)
            if final:
                y_p, y_s = x_all[:rows_p], x_all[rows_p:]
        else:
            i = l // 2
            splits = [(0, rows_p), (rows_p, rows_s)] if final else [(0, total)]
            outs = _moe_layer(x_all, splits, norm_ffn_g[l], moe_router_w[i], moe_router_b[i],
                              *moe_b, i, norm_final_g, final)
            if final:
                y_p, y_s = outs
            else:
                x_all = outs[0]

    return (y_p.reshape(batch, seq, d), y_s.reshape(dec_batch, dec_seq, d),
            new_mem_k, new_mem_v, jnp.stack(pool_p), jnp.stack(pool_s), jnp.stack(v_rows))
```

```python
import functools

import jax
import jax.numpy as jnp
from jax import lax
from jax.experimental import pallas as pl
from jax.experimental.pallas import tpu as pltpu

EPS = 1e-6
PAST_LEN = 4096
POOL_WINDOWS = (2, 4, 8, 16)
HIST_ROWS = 16
GMLP_CHUNK = 128
LANES = 128
SUBLANES = 8
TOP_K = 2

V7X_VMEM_BYTES = 64 * 1024 * 1024
VMEM_LIMIT = V7X_VMEM_BYTES - 8 * 1024 * 1024

MIX_ROWS = 256
ATTN_ROWS = 512
ATTN_SAMPLE_STREAMS = 8
FFN_ROWS = 512
FF_CHUNK = 256
MOVE_ROWS = 256

BF16 = jnp.bfloat16
F32 = jnp.float32


def _rmsnorm(x, g):
    return x * lax.rsqrt(jnp.mean(x * x, axis=-1, keepdims=True) + EPS) * g


def _dot(a, b):
    return jnp.dot(a, b, preferred_element_type=F32)


def _params(sem, vmem=VMEM_LIMIT):
    return pltpu.CompilerParams(dimension_semantics=sem, vmem_limit_bytes=vmem)


def _const_spec(shape, layer=None):
    nd = len(shape)
    if layer is None:
        return pl.BlockSpec(shape, lambda *_: (0,) * nd)
    return pl.BlockSpec((None,) + shape, lambda *_: (layer,) + (0,) * nd)


def _memkv_kernel(mem_ref, g_ref, wk_ref, wv_ref, k_ref, v_ref):
    m = _rmsnorm(mem_ref[...], g_ref[...]).astype(BF16)
    k_ref[...] = _dot(m, wk_ref[...])
    v_ref[...] = _dot(m, wv_ref[...])


def _memory_kv(mem2d, norm_g, wk, wv):
    depth, d = norm_g.shape
    rows = mem2d.shape[0]
    lay = lambda l: (l, 0, 0)
    return pl.pallas_call(
        _memkv_kernel,
        grid=(depth,),
        in_specs=[pl.BlockSpec((rows, d), lambda l: (0, 0)),
                  pl.BlockSpec((None, 1, d), lay),
                  pl.BlockSpec((None, d, d), lay),
                  pl.BlockSpec((None, d, d), lay)],
        out_specs=[pl.BlockSpec((None, rows, d), lay)] * 2,
        out_shape=[jax.ShapeDtypeStruct((depth, rows, d), F32)] * 2,
        compiler_params=_params(("arbitrary",)),
        name="memory_kv",
    )(mem2d, norm_g.reshape(depth, 1, d), wk, wv)


def _mixer_kernel(x_ref, g_ref, win_ref, poolw_ref, pscale_ref, lng_ref, lnb_ref, wbig_ref,
                  bias_ref, wout_ref, hist_ref, o_ref, hist_out_ref, *rest, n_seg, seg_rows, pos0,
                  keep_v):
    v_ref = rest[0] if keep_v else None
    hist_sc = rest[-1]
    j = pl.program_id(1)
    d_pool = pscale_ref.shape[-1]
    d_gate = lng_ref.shape[-1]
    pg = d_pool // len(POOL_WINDOWS)
    n_heads = wbig_ref.shape[0]
    gh = d_gate // n_heads

    @pl.when(j == 0)
    def _():
        hist_sc[...] = hist_ref[...]

    x = x_ref[...]
    h = _rmsnorm(x, g_ref[...]).astype(BF16)
    z = _dot(h, win_ref[...])
    p = z[:, :d_pool]
    u = jax.nn.gelu(z[:, d_pool:d_pool + d_gate])
    vpre = jax.nn.gelu(z[:, d_pool + d_gate:])
    mu = jnp.mean(vpre, axis=-1, keepdims=True)
    vc = vpre - mu
    var = jnp.mean(vc * vc, axis=-1, keepdims=True)
    v = vc * lax.rsqrt(var + EPS) * lng_ref[...] + lnb_ref[...]
    if keep_v:
        v_ref[...] = v

    pos = pos0 + j * seg_rows + lax.broadcasted_iota(jnp.int32, (seg_rows, pg), 0)
    pd_segs = []
    for s in range(n_seg):
        ps = p[s * seg_rows:(s + 1) * seg_rows]
        ext = jnp.concatenate([hist_sc[s], ps], axis=0)
        hist_sc[s] = ps[seg_rows - HIST_ROWS:]
        cols = []
        for gi, w in enumerate(POOL_WINDOWS):
            acc = ext[:, gi * pg:(gi + 1) * pg]
            span = 1
            while span < w:
                acc = acc + pltpu.roll(acc, span, 0)
                span *= 2
            cnt = jnp.minimum(pos + 1, w).astype(F32)
            cols.append(acc[HIST_ROWS:] / cnt - ps[:, gi * pg:(gi + 1) * pg])
        pd_segs.append(jnp.concatenate(cols, axis=1))
    pd = pd_segs[0] if n_seg == 1 else jnp.concatenate(pd_segs, axis=0)
    hist_out_ref[...] = hist_sc[...]

    pdb = pd.astype(BF16)
    parts = []
    for gi in range(len(POOL_WINDOWS)):
        sl = slice(gi * pg, (gi + 1) * pg)
        parts.append(_dot(pdb[:, sl], poolw_ref[gi]) * pscale_ref[:, sl])
    vb = v.astype(BF16)
    for hi in range(n_heads):
        sl = slice(hi * gh, (hi + 1) * gh)
        mixed = _dot(wbig_ref[hi], vb[:, sl]) + bias_ref[:, sl]
        parts.append(u[:, sl] * mixed)
    cat = jnp.concatenate(parts, axis=1).astype(BF16)
    o_ref[...] = x + _dot(cat, wout_ref[...])


def _mixer(x, in_place, n_b, n_j, n_seg, seg_rows, pos0, hist, layer, w, wbig, bias, keep_v):
    d = x.shape[1]
    rows = n_seg * seg_rows
    d_in = w["w_in"].shape[-1]
    d_pool = w["pool_scale"].shape[-1]
    d_gate = w["gm_ln_g"].shape[-1]
    vec = lambda a: a[layer].reshape(1, -1)
    xspec = pl.BlockSpec((rows, d), lambda b, j: (b * n_j + j, 0))
    hspec = pl.BlockSpec((n_seg, HIST_ROWS, d_pool), lambda b, j: (b, 0, 0))
    in_specs = [xspec, _const_spec((1, d)), _const_spec((d, d_in), layer),
                _const_spec(w["pool_w"].shape[1:], layer),
                _const_spec((1, d_pool)), _const_spec((1, d_gate)), _const_spec((1, d_gate)),
                _const_spec(wbig.shape), _const_spec(bias.shape), _const_spec((d, d), layer),
                hspec]
    out_specs = [xspec, hspec]
    out_shape = [jax.ShapeDtypeStruct(x.shape, F32), jax.ShapeDtypeStruct(hist.shape, F32)]
    if keep_v:
        out_specs.append(pl.BlockSpec((rows, d_gate), lambda b, j: (b * n_j + j, 0)))
        out_shape.append(jax.ShapeDtypeStruct((x.shape[0], d_gate), F32))
    return pl.pallas_call(
        functools.partial(_mixer_kernel, n_seg=n_seg, seg_rows=seg_rows, pos0=pos0, keep_v=keep_v),
        grid=(n_b, n_j),
        in_specs=in_specs,
        out_specs=out_specs,
        out_shape=out_shape,
        scratch_shapes=[pltpu.VMEM((n_seg, HIST_ROWS, d_pool), F32)],
        input_output_aliases={0: 0} if in_place else {},
        compiler_params=_params(("arbitrary", "arbitrary")),
        name="token_mixer",
    )(x, vec(w["norm_mix_g"]), w["w_in"], w["pool_w"], vec(w["pool_scale"]),
      vec(w["gm_ln_g"]), vec(w["gm_ln_b"]), wbig, bias, w["w_mix_out"], hist)


def _gate_operands(ws, bs, rows, chunk):
    n_heads = ws.shape[0]
    gh = GMLP_CHUNK
    tri = jnp.tril(jnp.ones((chunk, chunk), dtype=bool))
    wc = jnp.where(tri[None], ws[:, :chunk, :chunk], 0)
    eye = jnp.eye(rows // chunk, dtype=ws.dtype)
    wbig = jnp.einsum("ab,hts->hatbs", eye, wc).reshape(n_heads, rows, rows).astype(BF16)
    bias = jnp.tile(bs[:, :chunk].T, (rows // chunk, 1))
    bias = jnp.repeat(bias, gh, axis=1)
    return wbig, bias


def _attn_kernel(x_ref, g_ref, wq_ref, wo_ref, k_ref, v_ref, o_ref, *, n_heads):
    x = x_ref[...]
    d = x.shape[-1]
    hd = d // n_heads
    n_streams = k_ref.shape[0]
    seg = x.shape[0] // n_streams
    h = _rmsnorm(x, g_ref[...]).astype(BF16)
    q = (_dot(h, wq_ref[...]) * (hd ** -0.5)).astype(BF16)
    rows_out = []
    for st in range(n_streams):
        qs = q[st * seg:(st + 1) * seg]
        kb = k_ref[st].astype(BF16)
        vb = v_ref[st].astype(BF16)
        outs = []
        for hi in range(n_heads):
            sl = slice(hi * hd, (hi + 1) * hd)
            s = lax.dot_general(qs[:, sl], kb[:, sl], (((1,), (1,)), ((), ())),
                                preferred_element_type=F32)
            e = jnp.exp(s - jnp.max(s, axis=-1, keepdims=True))
            prob = e * (1.0 / jnp.sum(e, axis=-1, keepdims=True))
            outs.append(_dot(prob.astype(BF16), vb[:, sl]))
        rows_out.append(jnp.concatenate(outs, axis=1))
    o = rows_out[0] if n_streams == 1 else jnp.concatenate(rows_out, axis=0)
    o_ref[...] = x + _dot(o.astype(BF16), wo_ref[...])


def _attention(x, n_b, n_j, streams, mem_k, mem_v, layer, w, n_heads):
    d = x.shape[1]
    rows = x.shape[0] // (n_b * n_j)
    n_mem = mem_k.shape[2]
    xmap = lambda b, j: (b * n_j + j, 0)
    kvspec = pl.BlockSpec((None, streams, n_mem, d), lambda b, j: (layer, b, 0, 0))
    return pl.pallas_call(
        functools.partial(_attn_kernel, n_heads=n_heads),
        grid=(n_b, n_j),
        in_specs=[pl.BlockSpec((rows, d), xmap), _const_spec((1, d)),
                  _const_spec((d, d), layer), _const_spec((d, d), layer), kvspec, kvspec],
        out_specs=pl.BlockSpec((rows, d), xmap),
        out_shape=jax.ShapeDtypeStruct(x.shape, F32),
        input_output_aliases={0: 0},
        compiler_params=_params(("arbitrary", "arbitrary")),
        name="cross_attention",
    )(x, w["norm_xa_g"][layer].reshape(1, d), w["xa_wq"], w["xa_wo"], mem_k, mem_v)


def _swiglu_into(o_ref, hb, wg_ref, wu_ref, wd_ref, base):
    d_ff = wg_ref.shape[-1]
    for c in range(d_ff // FF_CHUNK):
        sl = slice(c * FF_CHUNK, (c + 1) * FF_CHUNK)
        a = (jax.nn.silu(_dot(hb, wg_ref[:, sl])) * _dot(hb, wu_ref[:, sl])).astype(BF16)
        part = _dot(a, wd_ref[sl, :])
        if c == 0:
            o_ref[...] = part if base is None else base + part
        else:
            o_ref[...] += part


def _ffn_kernel(x_ref, g_ref, wg_ref, wu_ref, wd_ref, gf_ref, o_ref, *, final):
    x = x_ref[...]
    hb = _rmsnorm(x, g_ref[...]).astype(BF16)
    _swiglu_into(o_ref, hb, wg_ref, wu_ref, wd_ref, x)
    if final:
        o_ref[...] = _rmsnorm(o_ref[...], gf_ref[...])


def _dense_ffn(x, g, wg, wu, wd, idx, g_final, final):
    t, d = x.shape
    d_ff = wg.shape[-1]
    assert t % FFN_ROWS == 0 and d_ff % FF_CHUNK == 0
    xspec = pl.BlockSpec((FFN_ROWS, d), lambda i: (i, 0))
    return pl.pallas_call(
        functools.partial(_ffn_kernel, final=final),
        grid=(t // FFN_ROWS,),
        in_specs=[xspec, _const_spec((1, d)), _const_spec((d, d_ff), idx),
                  _const_spec((d, d_ff), idx), _const_spec((d_ff, d), idx), _const_spec((1, d))],
        out_specs=xspec,
        out_shape=jax.ShapeDtypeStruct((t, d), F32),
        input_output_aliases={0: 0},
        compiler_params=_params(("arbitrary",)),
        name="dense_ffn",
    )(x, g.reshape(1, d), wg, wu, wd, g_final.reshape(1, d))


def _two_source_specs(rows, d, n_a):
    return [pl.BlockSpec((rows, d), lambda i, *_: (jnp.minimum(i, n_a - 1), 0)),
            pl.BlockSpec((rows, d), lambda i, *_: (jnp.maximum(i - n_a, 0), 0))]


def _router_kernel(xa_ref, xb_ref, g_ref, rw_ref, rb_ref, before_ref, sel_ref, gate_ref, cnt_ref,
                   carry, *, n_a):
    i = pl.program_id(0)

    @pl.when(i == 0)
    def _():
        carry[...] = jnp.zeros_like(carry)

    x = jnp.where(i < n_a, xa_ref[...], xb_ref[...])
    h = _rmsnorm(x, g_ref[...])
    logits = jnp.dot(h, rw_ref[...], preferred_element_type=F32,
                     precision=lax.Precision.HIGHEST) + rb_ref[...]
    lane_i = lax.broadcasted_iota(jnp.int32, logits.shape, 1)
    lane = lane_i.astype(F32)
    m0 = jnp.max(logits, axis=-1, keepdims=True)
    e0 = jnp.min(jnp.where(logits == m0, lane, float(LANES)), axis=-1, keepdims=True)
    rest = jnp.where(lane == e0, -jnp.inf, logits)
    m1 = jnp.max(rest, axis=-1, keepdims=True)
    e1 = jnp.min(jnp.where(rest == m1, lane, float(LANES)), axis=-1, keepdims=True)
    t = jnp.exp(m1 - m0)
    g0 = 1.0 / (1.0 + t)
    g1 = t * g0
    hot0 = (lane == e0).astype(F32)
    hot1 = (lane == e1).astype(F32)
    both = hot0 + hot1
    prior = _dot(before_ref[...], both.astype(BF16)) + carry[...]
    rank0 = jnp.sum(hot0 * prior, axis=-1, keepdims=True)
    rank1 = jnp.sum(hot1 * prior, axis=-1, keepdims=True)
    carry[...] += jnp.sum(both, axis=0, keepdims=True)
    cnt_ref[...] = carry[...]
    sel = jnp.where(lane_i == 0, e0, jnp.where(lane_i == 1, e1,
                    jnp.where(lane_i == 2, rank0, jnp.where(lane_i == 3, rank1, 0.0))))
    sel_ref[...] = jnp.transpose(sel)[:SUBLANES]
    gate_ref[...] = jnp.where(lane_i == 0, g0, jnp.where(lane_i == 1, g1, 0.0))


def _router(xa, xb, g, rw_pad, rb_pad):
    d = xa.shape[1]
    rows = FFN_ROWS
    n_a, n_b = xa.shape[0] // rows, xb.shape[0] // rows
    t = xa.shape[0] + xb.shape[0]
    before = jnp.tril(jnp.ones((rows, rows), F32), -1).astype(BF16)
    return pl.pallas_call(
        functools.partial(_router_kernel, n_a=n_a),
        grid=(n_a + n_b,),
        in_specs=_two_source_specs(rows, d, n_a) + [
            _const_spec((1, d)), _const_spec((d, LANES)), _const_spec((1, LANES)),
            _const_spec((rows, rows))],
        out_specs=[pl.BlockSpec((SUBLANES, rows), lambda i: (0, i)),
                   pl.BlockSpec((rows, LANES), lambda i: (i, 0)), _const_spec((1, LANES))],
        out_shape=[jax.ShapeDtypeStruct((SUBLANES, t), F32),
                   jax.ShapeDtypeStruct((t, LANES), F32),
                   jax.ShapeDtypeStruct((1, LANES), F32)],
        scratch_shapes=[pltpu.VMEM((1, LANES), F32)],
        compiler_params=_params(("arbitrary",)),
        name="moe_router",
    )(xa, xb, g.reshape(1, d), rw_pad, rb_pad, before)


def _dispatch_kernel(meta_ref, xa_ref, xb_ref, p0_ref, p1_ref, xs_ref, zero_sc, sem, zsem, *,
                     n_a, n_exp, min_tiles):
    i = pl.program_id(0)
    rows = xa_ref.shape[0]
    tile = zero_sc.shape[0]
    n_tiles = xs_ref.shape[0] // tile

    def zero_copy(row_end):
        start = pl.multiple_of(row_end - tile, tile)
        return pltpu.make_async_copy(zero_sc, xs_ref.at[pl.ds(start, tile)], zsem)

    fills = [(meta_ref[n_exp + e] > 0, meta_ref[e]) for e in range(n_exp)]
    fills += [(k >= meta_ref[2 * n_exp], (k + 1) * tile) for k in range(min_tiles, n_tiles)]

    @pl.when(i == 0)
    def _():
        zero_sc[...] = jnp.zeros_like(zero_sc)
        for needed, row_end in fills:
            @pl.when(needed)
            def _():
                zero_copy(row_end).start()
        for needed, row_end in fills:
            @pl.when(needed)
            def _():
                zero_copy(row_end).wait()

    def scatter_rows(x_ref):
        def start(grp, c):
            base = pl.multiple_of(grp * SUBLANES, SUBLANES)
            for u in range(SUBLANES):
                src = x_ref.at[pl.ds(base + u, 1)]
                pltpu.make_async_copy(src, xs_ref.at[pl.ds(p0_ref[base + u], 1)], sem).start()
                pltpu.make_async_copy(src, xs_ref.at[pl.ds(p1_ref[base + u], 1)], sem).start()
            return c

        lax.fori_loop(0, rows // SUBLANES, start, 0)
        for _ in range(TOP_K):
            pltpu.make_async_copy(x_ref, xs_ref.at[pl.ds(0, rows)], sem).wait()

    @pl.when(i < n_a)
    def _():
        scatter_rows(xa_ref)

    @pl.when(i >= n_a)
    def _():
        scatter_rows(xb_ref)


def _dispatch(xa, xb, pos0, pos1, meta, n_exp, n_sorted, tile):
    d = xa.shape[1]
    rows = MOVE_ROWS
    n_a, n_b = xa.shape[0] // rows, xb.shape[0] // rows
    t = xa.shape[0] + xb.shape[0]
    smem_rows = pl.BlockSpec((rows,), lambda i, *_: (i,), memory_space=pltpu.SMEM)
    min_tiles = -(-TOP_K * t // tile)
    return pl.pallas_call(
        functools.partial(_dispatch_kernel, n_a=n_a, n_exp=n_exp, min_tiles=min_tiles),
        grid_spec=pltpu.PrefetchScalarGridSpec(
            num_scalar_prefetch=1,
            grid=(n_a + n_b,),
            in_specs=_two_source_specs(rows, d, n_a) + [smem_rows, smem_rows],
            out_specs=pl.BlockSpec(memory_space=pl.ANY),
            scratch_shapes=[pltpu.VMEM((tile, d), F32), pltpu.SemaphoreType.DMA,
                            pltpu.SemaphoreType.DMA]),
        out_shape=jax.ShapeDtypeStruct((n_sorted, d), F32),
        compiler_params=_params(("arbitrary",)),
        name="moe_dispatch",
    )(meta, xa, xb, pos0, pos1)


def _expert_kernel(te_ref, nu_ref, xs_ref, g_ref, wg_ref, wu_ref, wd_ref, ys_ref):
    used = pl.program_id(0) < nu_ref[0]

    @pl.when(used)
    def _():
        hb = _rmsnorm(xs_ref[...], g_ref[...]).astype(BF16)
        _swiglu_into(ys_ref, hb, wg_ref, wu_ref, wd_ref, None)

    @pl.when(jnp.logical_not(used))
    def _():
        ys_ref[...] = jnp.zeros_like(ys_ref)


def _expert_ffn(xs, g, tile_expert, n_used, wg, wu, wd, idx, tile):
    n_sorted, d = xs.shape
    d_ff = wg.shape[-1]
    rowmap = lambda i, te, nu: (jnp.minimum(i, nu[0] - 1), 0)
    wmap = lambda i, te, nu: (idx, te[i], 0, 0)
    return pl.pallas_call(
        _expert_kernel,
        grid_spec=pltpu.PrefetchScalarGridSpec(
            num_scalar_prefetch=2,
            grid=(n_sorted // tile,),
            in_specs=[pl.BlockSpec((tile, d), rowmap),
                      pl.BlockSpec((1, d), lambda i, te, nu: (0, 0)),
                      pl.BlockSpec((None, None, d, d_ff), wmap),
                      pl.BlockSpec((None, None, d, d_ff), wmap),
                      pl.BlockSpec((None, None, d_ff, d), wmap)],
            out_specs=pl.BlockSpec((tile, d), lambda i, te, nu: (i, 0))),
        out_shape=jax.ShapeDtypeStruct((n_sorted, d), F32),
        compiler_params=_params(("arbitrary",)),
        name="expert_ffn",
    )(tile_expert, n_used, xs, g.reshape(1, d), wg, wu, wd)


def _combine_kernel(x_ref, p0_ref, p1_ref, gate_ref, gf_ref, ys_ref, o_ref, buf0, buf1, sem,
                    *, final):
    rows = x_ref.shape[0]

    def start(grp, c):
        base = pl.multiple_of(grp * SUBLANES, SUBLANES)
        for u in range(SUBLANES):
            r = base + u
            pltpu.make_async_copy(ys_ref.at[pl.ds(p0_ref[r], 1)], buf0.at[pl.ds(r, 1)], sem).start()
            pltpu.make_async_copy(ys_ref.at[pl.ds(p1_ref[r], 1)], buf1.at[pl.ds(r, 1)], sem).start()
        return c

    lax.fori_loop(0, rows // SUBLANES, start, 0)
    for buf in (buf0, buf1):
        pltpu.make_async_copy(ys_ref.at[pl.ds(0, rows)], buf, sem).wait()
    gates = gate_ref[...]
    out = x_ref[...] + gates[:, 0:1] * buf0[...] + gates[:, 1:2] * buf1[...]
    if final:
        out = _rmsnorm(out, gf_ref[...])
    o_ref[...] = out


def _combine(x, off_rows, pos0, pos1, gates, ys, g_final, final):
    n_rows, d = x.shape
    rows = MOVE_ROWS
    off = off_rows // rows
    smem_rows = pl.BlockSpec((rows,), lambda i: (off + i,), memory_space=pltpu.SMEM)
    xspec = pl.BlockSpec((rows, d), lambda i: (i, 0))
    return pl.pallas_call(
        functools.partial(_combine_kernel, final=final),
        grid=(n_rows // rows,),
        in_specs=[xspec, smem_rows, smem_rows,
                  pl.BlockSpec((rows, LANES), lambda i: (off + i, 0)), _const_spec((1, d)),
                  pl.BlockSpec(memory_space=pl.ANY)],
        out_specs=xspec,
        out_shape=jax.ShapeDtypeStruct((n_rows, d), F32),
        scratch_shapes=[pltpu.VMEM((rows, d), F32), pltpu.VMEM((rows, d), F32),
                        pltpu.SemaphoreType.DMA],
        compiler_params=_params(("arbitrary",)),
        name="moe_combine",
    )(x, pos0, pos1, gates, g_final.reshape(1, d), ys)


def _moe_layer(xa, xb, g, rw, rb, wg, wu, wd, idx, g_final, final):
    d = xa.shape[1]
    t = xa.shape[0] + xb.shape[0]
    n_exp = rw.shape[-1]
    tile = FFN_ROWS
    rw_pad = jnp.zeros((d, LANES), F32).at[:, :n_exp].set(rw)
    rb_pad = jnp.full((1, LANES), -jnp.inf, F32).at[0, :n_exp].set(rb)
    sel, gates, counts = _router(xa, xb, g, rw_pad, rb_pad)

    counts = counts[0, :n_exp].astype(jnp.int32)
    gpad = (counts + tile - 1) // tile * tile
    gend = jnp.cumsum(gpad)
    gstart = gend - gpad
    sel = sel.astype(jnp.int32)
    pos0 = gstart[sel[0]] + sel[2]
    pos1 = gstart[sel[1]] + sel[3]
    n_tiles = (TOP_K * t + n_exp * (tile - 1)) // tile
    tile_ids = jnp.arange(n_tiles, dtype=jnp.int32)
    tile_expert = jnp.minimum(
        jnp.sum((gend // tile)[None, :] <= tile_ids[:, None], axis=1), n_exp - 1).astype(jnp.int32)
    n_used = (gend[-1:] // tile).astype(jnp.int32)

    meta = jnp.concatenate([gend, gpad, n_used]).astype(jnp.int32)
    xs = _dispatch(xa, xb, pos0, pos1, meta, n_exp, n_tiles * tile, tile)
    ys = _expert_ffn(xs, g, tile_expert, n_used, wg, wu, wd, idx, tile)
    return (_combine(xa, 0, pos0, pos1, gates, ys, g_final, final),
            _combine(xb, xa.shape[0], pos0, pos1, gates, ys, g_final, final))


def kernel(x_prompt, x_sample, cache_mem_k, cache_mem_v, state_pool, mem_prompt, norm_mix_g, w_in, pool_w, pool_scale, gm_ln_g, gm_ln_b, gm_ws, gm_bs, w_mix_out, norm_xa_g, norm_mem_g, xa_wq, xa_wk, xa_wv, xa_wo, norm_ffn_g, ffn_wg, ffn_wu, ffn_wd, moe_router_w, moe_router_b, moe_wg, moe_wu, moe_wd, norm_final_g):
    batch, seq, d = x_prompt.shape
    dec_batch, dec_seq, _ = x_sample.shape
    depth = norm_mix_g.shape[0]
    n_mem, n_mem_heads, mem_hd = cache_mem_k.shape[2:]
    d_pool = pool_scale.shape[-1]
    d_gate = gm_ln_g.shape[-1]
    rows_p, rows_s = batch * seq, dec_batch * dec_seq
    seg_per_tile = MIX_ROWS // dec_seq
    assert seq % MIX_ROWS == 0 and seq % ATTN_ROWS == 0 and MIX_ROWS % GMLP_CHUNK == 0
    assert MIX_ROWS % dec_seq == 0 and dec_batch % seg_per_tile == 0 and dec_seq >= HIST_ROWS
    assert dec_seq <= GMLP_CHUNK and PAST_LEN % GMLP_CHUNK == 0
    assert dec_batch % ATTN_SAMPLE_STREAMS == 0
    assert rows_p % FFN_ROWS == 0 and rows_s % FFN_ROWS == 0

    w = dict(norm_mix_g=norm_mix_g, w_in=w_in.astype(BF16), pool_w=pool_w.astype(BF16),
             pool_scale=pool_scale, gm_ln_g=gm_ln_g, gm_ln_b=gm_ln_b,
             w_mix_out=w_mix_out.astype(BF16), norm_xa_g=norm_xa_g,
             xa_wq=xa_wq.astype(BF16), xa_wo=xa_wo.astype(BF16))
    ffn_b = [a.astype(BF16) for a in (ffn_wg, ffn_wu, ffn_wd)]
    moe_b = [a.astype(BF16) for a in (moe_wg, moe_wu, moe_wd)]

    mk, mv = _memory_kv(mem_prompt.reshape(batch * n_mem, d), norm_mem_g,
                        xa_wk.astype(BF16), xa_wv.astype(BF16))
    new_mem_k = mk.reshape(depth, batch, n_mem, n_mem_heads, mem_hd)
    new_mem_v = mv.reshape(depth, batch, n_mem, n_mem_heads, mem_hd)
    pk, pv = mk.reshape(depth, batch, n_mem, d), mv.reshape(depth, batch, n_mem, d)
    ck = cache_mem_k.astype(BF16).reshape(depth, dec_batch, n_mem, d)
    cv = cache_mem_v.astype(BF16).reshape(depth, dec_batch, n_mem, d)

    hist_p0 = jnp.zeros((batch, HIST_ROWS, d_pool), F32)
    hist_s = jnp.pad(state_pool, ((0, 0), (0, 0), (HIST_ROWS - state_pool.shape[2], 0), (0, 0)))
    keep = HIST_ROWS - state_pool.shape[2]

    xp = x_prompt.reshape(rows_p, d)
    xs = x_sample.reshape(rows_s, d)
    pool_p, pool_s, v_rows = [], [], []
    for l in range(depth):
        wbig_p, bias_p = _gate_operands(gm_ws[l], gm_bs[l], MIX_ROWS, GMLP_CHUNK)
        wbig_s, bias_s = _gate_operands(gm_ws[l], gm_bs[l], MIX_ROWS, dec_seq)
        xp, hp = _mixer(xp, l > 0, batch, seq // MIX_ROWS, 1, MIX_ROWS, 0, hist_p0, l, w,
                        wbig_p, bias_p, False)
        xs, hs, vr = _mixer(xs, l > 0, dec_batch // seg_per_tile, 1, seg_per_tile, dec_seq,
                            PAST_LEN, hist_s[l], l, w, wbig_s, bias_s, True)
        pool_p.append(hp[:, keep:])
        pool_s.append(hs[:, keep:])
        v_rows.append(vr.reshape(dec_batch, dec_seq, d_gate))

        xp = _attention(xp, batch, seq // ATTN_ROWS, 1, pk, pv, l, w, n_mem_heads)
        xs = _attention(xs, dec_batch // ATTN_SAMPLE_STREAMS, 1, ATTN_SAMPLE_STREAMS, ck, cv, l, w,
                        n_mem_heads)

        final = l == depth - 1
        if l % 2 == 0:
            xp = _dense_ffn(xp, norm_ffn_g[l], *ffn_b, l // 2, norm_final_g, final)
            xs = _dense_ffn(xs, norm_ffn_g[l], *ffn_b, l // 2, norm_final_g, final)
        else:
            i = l // 2
            xp, xs = _moe_layer(xp, xs, norm_ffn_g[l], moe_router_w[i], moe_router_b[i], *moe_b, i,
                                norm_final_g, final)

    return (xp.reshape(batch, seq, d), xs.reshape(dec_batch, dec_seq, d),
            new_mem_k, new_mem_v, jnp.stack(pool_p), jnp.stack(pool_s), jnp.stack(v_rows))
```

```python
import functools

import jax
import jax.numpy as jnp
from jax import lax
from jax.experimental import pallas as pl
from jax.experimental.pallas import tpu as pltpu

EPS = 1e-6
PAST_LEN = 4096
POOL_WINDOWS = (2, 4, 8, 16)
HIST_ROWS = 16
GMLP_CHUNK = 128
LANES = 128
SUBLANES = 8
TOP_K = 2

V7X_VMEM_BYTES = 64 * 1024 * 1024
VMEM_LIMIT = V7X_VMEM_BYTES - 8 * 1024 * 1024

MIX_ROWS = 256
ATTN_ROWS = 512
ATTN_SAMPLE_STREAMS = 8
FFN_ROWS = 512
FF_CHUNK = 256
MOVE_ROWS = 256

BF16 = jnp.bfloat16
F32 = jnp.float32


def _rmsnorm(x, g):
    return x * lax.rsqrt(jnp.mean(x * x, axis=-1, keepdims=True) + EPS) * g


def _dot(a, b):
    return jnp.dot(a, b, preferred_element_type=F32)


def _params(sem, vmem=VMEM_LIMIT):
    return pltpu.CompilerParams(dimension_semantics=sem, vmem_limit_bytes=vmem)


def _const_spec(shape, layer=None):
    nd = len(shape)
    if layer is None:
        return pl.BlockSpec(shape, lambda *_: (0,) * nd)
    return pl.BlockSpec((None,) + shape, lambda *_: (layer,) + (0,) * nd)


def _memkv_kernel(mem_ref, g_ref, wk_ref, wv_ref, k_ref, v_ref):
    m = _rmsnorm(mem_ref[...], g_ref[...]).astype(BF16)
    k_ref[...] = _dot(m, wk_ref[...])
    v_ref[...] = _dot(m, wv_ref[...])


def _memory_kv(mem2d, norm_g, wk, wv):
    depth, d = norm_g.shape
    rows = mem2d.shape[0]
    lay = lambda l: (l, 0, 0)
    return pl.pallas_call(
        _memkv_kernel,
        grid=(depth,),
        in_specs=[pl.BlockSpec((rows, d), lambda l: (0, 0)),
                  pl.BlockSpec((None, 1, d), lay),
                  pl.BlockSpec((None, d, d), lay),
                  pl.BlockSpec((None, d, d), lay)],
        out_specs=[pl.BlockSpec((None, rows, d), lay)] * 2,
        out_shape=[jax.ShapeDtypeStruct((depth, rows, d), F32)] * 2,
        compiler_params=_params(("arbitrary",)),
        name="memory_kv",
    )(mem2d, norm_g.reshape(depth, 1, d), wk, wv)


def _mixer_kernel(x_ref, g_ref, win_ref, poolw_ref, pscale_ref, lng_ref, lnb_ref, wbig_ref,
                  bias_ref, wout_ref, hist_ref, o_ref, hist_out_ref, *rest, n_seg, seg_rows, pos0,
                  keep_v):
    v_ref = rest[0] if keep_v else None
    hist_sc = rest[-1]
    j = pl.program_id(1)
    d_pool = pscale_ref.shape[-1]
    d_gate = lng_ref.shape[-1]
    pg = d_pool // len(POOL_WINDOWS)
    n_heads = wbig_ref.shape[0]
    gh = d_gate // n_heads

    @pl.when(j == 0)
    def _():
        hist_sc[...] = hist_ref[...]

    x = x_ref[...]
    h = _rmsnorm(x, g_ref[...]).astype(BF16)
    z = _dot(h, win_ref[...])
    p = z[:, :d_pool]
    u = jax.nn.gelu(z[:, d_pool:d_pool + d_gate])
    vpre = jax.nn.gelu(z[:, d_pool + d_gate:])
    mu = jnp.mean(vpre, axis=-1, keepdims=True)
    vc = vpre - mu
    var = jnp.mean(vc * vc, axis=-1, keepdims=True)
    v = vc * lax.rsqrt(var + EPS) * lng_ref[...] + lnb_ref[...]
    if keep_v:
        v_ref[...] = v

    pos = pos0 + j * seg_rows + lax.broadcasted_iota(jnp.int32, (seg_rows, pg), 0)
    pd_segs = []
    for s in range(n_seg):
        ps = p[s * seg_rows:(s + 1) * seg_rows]
        ext = jnp.concatenate([hist_sc[s], ps], axis=0)
        hist_sc[s] = ps[seg_rows - HIST_ROWS:]
        cols = []
        for gi, w in enumerate(POOL_WINDOWS):
            acc = ext[:, gi * pg:(gi + 1) * pg]
            span = 1
            while span < w:
                acc = acc + pltpu.roll(acc, span, 0)
                span *= 2
            cnt = jnp.minimum(pos + 1, w).astype(F32)
            cols.append(acc[HIST_ROWS:] / cnt - ps[:, gi * pg:(gi + 1) * pg])
        pd_segs.append(jnp.concatenate(cols, axis=1))
    pd = pd_segs[0] if n_seg == 1 else jnp.concatenate(pd_segs, axis=0)
    hist_out_ref[...] = hist_sc[...]

    pdb = pd.astype(BF16)
    parts = []
    for gi in range(len(POOL_WINDOWS)):
        sl = slice(gi * pg, (gi + 1) * pg)
        parts.append(_dot(pdb[:, sl], poolw_ref[gi]) * pscale_ref[:, sl])
    vb = v.astype(BF16)
    for hi in range(n_heads):
        sl = slice(hi * gh, (hi + 1) * gh)
        mixed = _dot(wbig_ref[hi], vb[:, sl]) + bias_ref[:, sl]
        parts.append(u[:, sl] * mixed)
    cat = jnp.concatenate(parts, axis=1).astype(BF16)
    o_ref[...] = x + _dot(cat, wout_ref[...])


def _mixer(x, in_place, n_b, n_j, n_seg, seg_rows, pos0, hist, layer, w, wbig, bias, keep_v):
    d = x.shape[1]
    rows = n_seg * seg_rows
    d_in = w["w_in"].shape[-1]
    d_pool = w["pool_scale"].shape[-1]
    d_gate = w["gm_ln_g"].shape[-1]
    vec = lambda a: a[layer].reshape(1, -1)
    xspec = pl.BlockSpec((rows, d), lambda b, j: (b * n_j + j, 0))
    hspec = pl.BlockSpec((n_seg, HIST_ROWS, d_pool), lambda b, j: (b, 0, 0))
    in_specs = [xspec, _const_spec((1, d)), _const_spec((d, d_in), layer),
                _const_spec(w["pool_w"].shape[1:], layer),
                _const_spec((1, d_pool)), _const_spec((1, d_gate)), _const_spec((1, d_gate)),
                _const_spec(wbig.shape), _const_spec(bias.shape), _const_spec((d, d), layer),
                hspec]
    out_specs = [xspec, hspec]
    out_shape = [jax.ShapeDtypeStruct(x.shape, F32), jax.ShapeDtypeStruct(hist.shape, F32)]
    if keep_v:
        out_specs.append(pl.BlockSpec((rows, d_gate), lambda b, j: (b * n_j + j, 0)))
        out_shape.append(jax.ShapeDtypeStruct((x.shape[0], d_gate), F32))
    return pl.pallas_call(
        functools.partial(_mixer_kernel, n_seg=n_seg, seg_rows=seg_rows, pos0=pos0, keep_v=keep_v),
        grid=(n_b, n_j),
        in_specs=in_specs,
        out_specs=out_specs,
        out_shape=out_shape,
        scratch_shapes=[pltpu.VMEM((n_seg, HIST_ROWS, d_pool), F32)],
        input_output_aliases={0: 0} if in_place else {},
        compiler_params=_params(("arbitrary", "arbitrary")),
        name="token_mixer",
    )(x, vec(w["norm_mix_g"]), w["w_in"], w["pool_w"], vec(w["pool_scale"]),
      vec(w["gm_ln_g"]), vec(w["gm_ln_b"]), wbig, bias, w["w_mix_out"], hist)


def _gate_operands(ws, bs, rows, chunk):
    n_heads = ws.shape[0]
    gh = GMLP_CHUNK
    tri = jnp.tril(jnp.ones((chunk, chunk), dtype=bool))
    wc = jnp.where(tri[None], ws[:, :chunk, :chunk], 0)
    eye = jnp.eye(rows // chunk, dtype=ws.dtype)
    wbig = jnp.einsum("ab,hts->hatbs", eye, wc).reshape(n_heads, rows, rows).astype(BF16)
    bias = jnp.tile(bs[:, :chunk].T, (rows // chunk, 1))
    bias = jnp.repeat(bias, gh, axis=1)
    return wbig, bias


def _queries(x, g_ref, wq_ref, hd):
    h = _rmsnorm(x, g_ref[...]).astype(BF16)
    return (_dot(h, wq_ref[...]) * (hd ** -0.5)).astype(BF16)


def _attend(qs, k_head, v_head, n_heads, hd):
    outs = []
    for hi in range(n_heads):
        s = lax.dot_general(qs[:, hi * hd:(hi + 1) * hd], k_head(hi), (((1,), (1,)), ((), ())),
                            preferred_element_type=F32)
        e = jnp.exp(s - jnp.max(s, axis=-1, keepdims=True))
        prob = e * (1.0 / jnp.sum(e, axis=-1, keepdims=True))
        outs.append(_dot(prob.astype(BF16), v_head(hi)))
    return jnp.concatenate(outs, axis=1)


def _attn_kernel(x_ref, g_ref, wq_ref, wo_ref, k_ref, v_ref, o_ref, *, n_heads):
    x = x_ref[...]
    hd = x.shape[-1] // n_heads
    q = _queries(x, g_ref, wq_ref, hd)
    kb = k_ref[...].astype(BF16)
    vb = v_ref[...].astype(BF16)
    o = _attend(q, lambda hi: kb[:, hi * hd:(hi + 1) * hd], lambda hi: vb[:, hi * hd:(hi + 1) * hd],
                n_heads, hd)
    o_ref[...] = x + _dot(o.astype(BF16), wo_ref[...])


def _attention(x, n_b, n_j, mem_k, mem_v, layer, w, n_heads):
    d = x.shape[1]
    rows = x.shape[0] // (n_b * n_j)
    n_mem = mem_k.shape[2]
    xmap = lambda b, j: (b * n_j + j, 0)
    kvspec = pl.BlockSpec((None, None, n_mem, d), lambda b, j: (layer, b, 0, 0))
    return pl.pallas_call(
        functools.partial(_attn_kernel, n_heads=n_heads),
        grid=(n_b, n_j),
        in_specs=[pl.BlockSpec((rows, d), xmap), _const_spec((1, d)),
                  _const_spec((d, d), layer), _const_spec((d, d), layer), kvspec, kvspec],
        out_specs=pl.BlockSpec((rows, d), xmap),
        out_shape=jax.ShapeDtypeStruct(x.shape, F32),
        input_output_aliases={0: 0},
        compiler_params=_params(("arbitrary", "arbitrary")),
        name="cross_attention",
    )(x, w["norm_xa_g"][layer].reshape(1, d), w["xa_wq"], w["xa_wo"], mem_k, mem_v)


def _attn_cached_kernel(x_ref, g_ref, wq_ref, wo_ref, k_hbm, v_hbm, o_ref, kbuf, vbuf, sem, *,
                        layer, streams):
    i = pl.program_id(0)
    n_heads, hd = k_hbm.shape[-2:]
    slot = i % 2

    def copies(step, to_slot):
        src = pl.ds(step * streams, streams)
        return [pltpu.make_async_copy(hbm.at[layer, src, :, hi, :], buf.at[to_slot, hi],
                                      sem.at[to_slot])
                for hbm, buf in ((k_hbm, kbuf), (v_hbm, vbuf)) for hi in range(n_heads)]

    @pl.when(i == 0)
    def _():
        for c in copies(0, 0):
            c.start()

    @pl.when(i + 1 < pl.num_programs(0))
    def _():
        for c in copies(i + 1, 1 - slot):
            c.start()

    x = x_ref[...]
    q = _queries(x, g_ref, wq_ref, hd)
    for c in copies(i, slot):
        c.wait()
    seg = x.shape[0] // streams
    outs = []
    for hi in range(n_heads):
        qh = q[:, hi * hd:(hi + 1) * hd].reshape(streams, seg, hd)
        s = jnp.einsum("sld,smd->slm", qh, kbuf[slot, hi].astype(BF16),
                       preferred_element_type=F32)
        e = jnp.exp(s - jnp.max(s, axis=-1, keepdims=True))
        prob = e * (1.0 / jnp.sum(e, axis=-1, keepdims=True))
        oh = jnp.einsum("slm,smd->sld", prob.astype(BF16), vbuf[slot, hi].astype(BF16),
                        preferred_element_type=F32)
        outs.append(oh.reshape(streams * seg, hd))
    o = jnp.concatenate(outs, axis=1).astype(BF16)
    o_ref[...] = x + _dot(o, wo_ref[...])


def _attention_cached(x, streams, cache_k, cache_v, layer, w):
    d = x.shape[1]
    n_streams, n_mem, n_heads, hd = cache_k.shape[1:]
    rows = x.shape[0] // n_streams * streams
    xspec = pl.BlockSpec((rows, d), lambda i: (i, 0))
    hbm = pl.BlockSpec(memory_space=pl.ANY)
    buf = pltpu.VMEM((2, n_heads, streams, n_mem, hd), F32)
    return pl.pallas_call(
        functools.partial(_attn_cached_kernel, layer=layer, streams=streams),
        grid=(n_streams // streams,),
        in_specs=[xspec, _const_spec((1, d)), _const_spec((d, d), layer),
                  _const_spec((d, d), layer), hbm, hbm],
        out_specs=xspec,
        out_shape=jax.ShapeDtypeStruct(x.shape, F32),
        scratch_shapes=[buf, buf, pltpu.SemaphoreType.DMA((2,))],
        input_output_aliases={0: 0},
        compiler_params=_params(("arbitrary",)),
        name="cached_attention",
    )(x, w["norm_xa_g"][layer].reshape(1, d), w["xa_wq"], w["xa_wo"], cache_k, cache_v)


def _swiglu_into(o_ref, hb, wg_ref, wu_ref, wd_ref, base):
    d_ff = wg_ref.shape[-1]
    for c in range(d_ff // FF_CHUNK):
        sl = slice(c * FF_CHUNK, (c + 1) * FF_CHUNK)
        a = (jax.nn.silu(_dot(hb, wg_ref[:, sl])) * _dot(hb, wu_ref[:, sl])).astype(BF16)
        part = _dot(a, wd_ref[sl, :])
        if c == 0:
            o_ref[...] = part if base is None else base + part
        else:
            o_ref[...] += part


def _ffn_kernel(x_ref, g_ref, wg_ref, wu_ref, wd_ref, gf_ref, o_ref, *, final):
    x = x_ref[...]
    hb = _rmsnorm(x, g_ref[...]).astype(BF16)
    _swiglu_into(o_ref, hb, wg_ref, wu_ref, wd_ref, x)
    if final:
        o_ref[...] = _rmsnorm(o_ref[...], gf_ref[...])


def _dense_ffn(x, g, wg, wu, wd, idx, g_final, final):
    t, d = x.shape
    d_ff = wg.shape[-1]
    assert t % FFN_ROWS == 0 and d_ff % FF_CHUNK == 0
    xspec = pl.BlockSpec((FFN_ROWS, d), lambda i: (i, 0))
    return pl.pallas_call(
        functools.partial(_ffn_kernel, final=final),
        grid=(t // FFN_ROWS,),
        in_specs=[xspec, _const_spec((1, d)), _const_spec((d, d_ff), idx),
                  _const_spec((d, d_ff), idx), _const_spec((d_ff, d), idx), _const_spec((1, d))],
        out_specs=xspec,
        out_shape=jax.ShapeDtypeStruct((t, d), F32),
        input_output_aliases={0: 0},
        compiler_params=_params(("arbitrary",)),
        name="dense_ffn",
    )(x, g.reshape(1, d), wg, wu, wd, g_final.reshape(1, d))


def _two_source_specs(rows, d, n_a):
    return [pl.BlockSpec((rows, d), lambda i, *_: (jnp.minimum(i, n_a - 1), 0)),
            pl.BlockSpec((rows, d), lambda i, *_: (jnp.maximum(i - n_a, 0), 0))]


def _router_kernel(xa_ref, xb_ref, g_ref, rw_ref, rb_ref, before_ref, sel_ref, gate_ref, cnt_ref,
                   carry, *, n_a):
    i = pl.program_id(0)

    @pl.when(i == 0)
    def _():
        carry[...] = jnp.zeros_like(carry)

    x = jnp.where(i < n_a, xa_ref[...], xb_ref[...])
    h = _rmsnorm(x, g_ref[...])
    rw = rw_ref[...]
    h_hi, w_hi = h.astype(BF16), rw.astype(BF16)
    h_lo = (h - h_hi.astype(F32)).astype(BF16)
    w_lo = (rw - w_hi.astype(F32)).astype(BF16)
    logits = _dot(h_hi, w_hi) + (_dot(h_hi, w_lo) + _dot(h_lo, w_hi)) + rb_ref[...]
    lane_i = lax.broadcasted_iota(jnp.int32, logits.shape, 1)
    lane = lane_i.astype(F32)
    m0 = jnp.max(logits, axis=-1, keepdims=True)
    e0 = jnp.min(jnp.where(logits == m0, lane, float(LANES)), axis=-1, keepdims=True)
    rest = jnp.where(lane == e0, -jnp.inf, logits)
    m1 = jnp.max(rest, axis=-1, keepdims=True)
    e1 = jnp.min(jnp.where(rest == m1, lane, float(LANES)), axis=-1, keepdims=True)
    t = jnp.exp(m1 - m0)
    g0 = 1.0 / (1.0 + t)
    g1 = t * g0
    hot0 = (lane == e0).astype(F32)
    hot1 = (lane == e1).astype(F32)
    both = hot0 + hot1
    prior = _dot(before_ref[...], both.astype(BF16)) + carry[...]
    rank0 = jnp.sum(hot0 * prior, axis=-1, keepdims=True)
    rank1 = jnp.sum(hot1 * prior, axis=-1, keepdims=True)
    carry[...] += jnp.sum(both, axis=0, keepdims=True)
    cnt_ref[...] = carry[...]
    sel = jnp.where(lane_i == 0, e0, jnp.where(lane_i == 1, e1,
                    jnp.where(lane_i == 2, rank0, jnp.where(lane_i == 3, rank1, 0.0))))
    sel_ref[...] = jnp.transpose(sel)[:SUBLANES]
    gate_ref[...] = jnp.where(lane_i == 0, g0, jnp.where(lane_i == 1, g1, 0.0))


def _router(xa, xb, g, rw_pad, rb_pad):
    d = xa.shape[1]
    rows = FFN_ROWS
    n_a, n_b = xa.shape[0] // rows, xb.shape[0] // rows
    t = xa.shape[0] + xb.shape[0]
    before = jnp.tril(jnp.ones((rows, rows), F32), -1).astype(BF16)
    return pl.pallas_call(
        functools.partial(_router_kernel, n_a=n_a),
        grid=(n_a + n_b,),
        in_specs=_two_source_specs(rows, d, n_a) + [
            _const_spec((1, d)), _const_spec((d, LANES)), _const_spec((1, LANES)),
            _const_spec((rows, rows))],
        out_specs=[pl.BlockSpec((SUBLANES, rows), lambda i: (0, i)),
                   pl.BlockSpec((rows, LANES), lambda i: (i, 0)), _const_spec((1, LANES))],
        out_shape=[jax.ShapeDtypeStruct((SUBLANES, t), F32),
                   jax.ShapeDtypeStruct((t, LANES), F32),
                   jax.ShapeDtypeStruct((1, LANES), F32)],
        scratch_shapes=[pltpu.VMEM((1, LANES), F32)],
        compiler_params=_params(("arbitrary",)),
        name="moe_router",
    )(xa, xb, g.reshape(1, d), rw_pad, rb_pad, before)


def _dispatch_kernel(meta_ref, xa_ref, xb_ref, p0_ref, p1_ref, xs_ref, zero_sc, sem, zsem, *,
                     n_a, n_exp, min_tiles):
    i = pl.program_id(0)
    rows = xa_ref.shape[0]
    tile = zero_sc.shape[0]
    n_tiles = xs_ref.shape[0] // tile

    def zero_copy(row_end):
        start = pl.multiple_of(row_end - tile, tile)
        return pltpu.make_async_copy(zero_sc, xs_ref.at[pl.ds(start, tile)], zsem)

    fills = [(meta_ref[n_exp + e] > 0, meta_ref[e]) for e in range(n_exp)]
    fills += [(k >= meta_ref[2 * n_exp], (k + 1) * tile) for k in range(min_tiles, n_tiles)]

    @pl.when(i == 0)
    def _():
        zero_sc[...] = jnp.zeros_like(zero_sc)
        for needed, row_end in fills:
            @pl.when(needed)
            def _():
                zero_copy(row_end).start()
        for needed, row_end in fills:
            @pl.when(needed)
            def _():
                zero_copy(row_end).wait()

    def scatter_rows(x_ref):
        def start(grp, c):
            base = pl.multiple_of(grp * SUBLANES, SUBLANES)
            for u in range(SUBLANES):
                src = x_ref.at[pl.ds(base + u, 1)]
                pltpu.make_async_copy(src, xs_ref.at[pl.ds(p0_ref[base + u], 1)], sem).start()
                pltpu.make_async_copy(src, xs_ref.at[pl.ds(p1_ref[base + u], 1)], sem).start()
            return c

        lax.fori_loop(0, rows // SUBLANES, start, 0)
        for _ in range(TOP_K):
            pltpu.make_async_copy(x_ref, xs_ref.at[pl.ds(0, rows)], sem).wait()

    @pl.when(i < n_a)
    def _():
        scatter_rows(xa_ref)

    @pl.when(i >= n_a)
    def _():
        scatter_rows(xb_ref)


def _dispatch(xa, xb, pos0, pos1, meta, n_exp, n_sorted, tile):
    d = xa.shape[1]
    rows = MOVE_ROWS
    n_a, n_b = xa.shape[0] // rows, xb.shape[0] // rows
    t = xa.shape[0] + xb.shape[0]
    smem_rows = pl.BlockSpec((rows,), lambda i, *_: (i,), memory_space=pltpu.SMEM)
    min_tiles = -(-TOP_K * t // tile)
    return pl.pallas_call(
        functools.partial(_dispatch_kernel, n_a=n_a, n_exp=n_exp, min_tiles=min_tiles),
        grid_spec=pltpu.PrefetchScalarGridSpec(
            num_scalar_prefetch=1,
            grid=(n_a + n_b,),
            in_specs=_two_source_specs(rows, d, n_a) + [smem_rows, smem_rows],
            out_specs=pl.BlockSpec(memory_space=pl.ANY),
            scratch_shapes=[pltpu.VMEM((tile, d), F32), pltpu.SemaphoreType.DMA,
                            pltpu.SemaphoreType.DMA]),
        out_shape=jax.ShapeDtypeStruct((n_sorted, d), F32),
        compiler_params=_params(("arbitrary",)),
        name="moe_dispatch",
    )(meta, xa, xb, pos0, pos1)


def _expert_kernel(te_ref, nu_ref, xs_ref, g_ref, wg_ref, wu_ref, wd_ref, ys_ref):
    used = pl.program_id(0) < nu_ref[0]

    @pl.when(used)
    def _():
        hb = _rmsnorm(xs_ref[...], g_ref[...]).astype(BF16)
        _swiglu_into(ys_ref, hb, wg_ref, wu_ref, wd_ref, None)

    @pl.when(jnp.logical_not(used))
    def _():
        ys_ref[...] = jnp.zeros_like(ys_ref)


def _expert_ffn(xs, g, tile_expert, n_used, wg, wu, wd, idx, tile):
    n_sorted, d = xs.shape
    d_ff = wg.shape[-1]
    rowmap = lambda i, te, nu: (jnp.minimum(i, nu[0] - 1), 0)
    wmap = lambda i, te, nu: (idx, te[i], 0, 0)
    return pl.pallas_call(
        _expert_kernel,
        grid_spec=pltpu.PrefetchScalarGridSpec(
            num_scalar_prefetch=2,
            grid=(n_sorted // tile,),
            in_specs=[pl.BlockSpec((tile, d), rowmap),
                      pl.BlockSpec((1, d), lambda i, te, nu: (0, 0)),
                      pl.BlockSpec((None, None, d, d_ff), wmap),
                      pl.BlockSpec((None, None, d, d_ff), wmap),
                      pl.BlockSpec((None, None, d_ff, d), wmap)],
            out_specs=pl.BlockSpec((tile, d), lambda i, te, nu: (i, 0))),
        out_shape=jax.ShapeDtypeStruct((n_sorted, d), F32),
        compiler_params=_params(("arbitrary",)),
        name="expert_ffn",
    )(tile_expert, n_used, xs, g.reshape(1, d), wg, wu, wd)


def _combine_kernel(x_ref, p0_ref, p1_ref, gate_ref, gf_ref, ys_ref, o_ref, buf0, buf1, sem,
                    *, final):
    rows = x_ref.shape[0]

    def start(grp, c):
        base = pl.multiple_of(grp * SUBLANES, SUBLANES)
        for u in range(SUBLANES):
            r = base + u
            pltpu.make_async_copy(ys_ref.at[pl.ds(p0_ref[r], 1)], buf0.at[pl.ds(r, 1)], sem).start()
            pltpu.make_async_copy(ys_ref.at[pl.ds(p1_ref[r], 1)], buf1.at[pl.ds(r, 1)], sem).start()
        return c

    lax.fori_loop(0, rows // SUBLANES, start, 0)
    for buf in (buf0, buf1):
        pltpu.make_async_copy(ys_ref.at[pl.ds(0, rows)], buf, sem).wait()
    gates = gate_ref[...]
    out = x_ref[...] + gates[:, 0:1] * buf0[...] + gates[:, 1:2] * buf1[...]
    if final:
        out = _rmsnorm(out, gf_ref[...])
    o_ref[...] = out


def _combine(x, off_rows, pos0, pos1, gates, ys, g_final, final):
    n_rows, d = x.shape
    rows = MOVE_ROWS
    off = off_rows // rows
    smem_rows = pl.BlockSpec((rows,), lambda i: (off + i,), memory_space=pltpu.SMEM)
    xspec = pl.BlockSpec((rows, d), lambda i: (i, 0))
    return pl.pallas_call(
        functools.partial(_combine_kernel, final=final),
        grid=(n_rows // rows,),
        in_specs=[xspec, smem_rows, smem_rows,
                  pl.BlockSpec((rows, LANES), lambda i: (off + i, 0)), _const_spec((1, d)),
                  pl.BlockSpec(memory_space=pl.ANY)],
        out_specs=xspec,
        out_shape=jax.ShapeDtypeStruct((n_rows, d), F32),
        scratch_shapes=[pltpu.VMEM((rows, d), F32), pltpu.VMEM((rows, d), F32),
                        pltpu.SemaphoreType.DMA],
        compiler_params=_params(("arbitrary",)),
        name="moe_combine",
    )(x, pos0, pos1, gates, g_final.reshape(1, d), ys)


def _moe_layer(xa, xb, g, rw, rb, wg, wu, wd, idx, g_final, final):
    d = xa.shape[1]
    t = xa.shape[0] + xb.shape[0]
    n_exp = rw.shape[-1]
    tile = FFN_ROWS
    rw_pad = jnp.zeros((d, LANES), F32).at[:, :n_exp].set(rw)
    rb_pad = jnp.full((1, LANES), -jnp.inf, F32).at[0, :n_exp].set(rb)
    sel, gates, counts = _router(xa, xb, g, rw_pad, rb_pad)

    counts = counts[0, :n_exp].astype(jnp.int32)
    gpad = (counts + tile - 1) // tile * tile
    gend = jnp.cumsum(gpad)
    gstart = gend - gpad
    sel = sel.astype(jnp.int32)
    pos0 = gstart[sel[0]] + sel[2]
    pos1 = gstart[sel[1]] + sel[3]
    n_tiles = (TOP_K * t + n_exp * (tile - 1)) // tile
    tile_ids = jnp.arange(n_tiles, dtype=jnp.int32)
    tile_expert = jnp.minimum(
        jnp.sum((gend // tile)[None, :] <= tile_ids[:, None], axis=1), n_exp - 1).astype(jnp.int32)
    n_used = (gend[-1:] // tile).astype(jnp.int32)

    meta = jnp.concatenate([gend, gpad, n_used]).astype(jnp.int32)
    xs = _dispatch(xa, xb, pos0, pos1, meta, n_exp, n_tiles * tile, tile)
    ys = _expert_ffn(xs, g, tile_expert, n_used, wg, wu, wd, idx, tile)
    return (_combine(xa, 0, pos0, pos1, gates, ys, g_final, final),
            _combine(xb, xa.shape[0], pos0, pos1, gates, ys, g_final, final))


def kernel(x_prompt, x_sample, cache_mem_k, cache_mem_v, state_pool, mem_prompt, norm_mix_g, w_in, pool_w, pool_scale, gm_ln_g, gm_ln_b, gm_ws, gm_bs, w_mix_out, norm_xa_g, norm_mem_g, xa_wq, xa_wk, xa_wv, xa_wo, norm_ffn_g, ffn_wg, ffn_wu, ffn_wd, moe_router_w, moe_router_b, moe_wg, moe_wu, moe_wd, norm_final_g):
    batch, seq, d = x_prompt.shape
    dec_batch, dec_seq, _ = x_sample.shape
    depth = norm_mix_g.shape[0]
    n_mem, n_mem_heads, mem_hd = cache_mem_k.shape[2:]
    d_pool = pool_scale.shape[-1]
    d_gate = gm_ln_g.shape[-1]
    rows_p, rows_s = batch * seq, dec_batch * dec_seq
    seg_per_tile = MIX_ROWS // dec_seq
    assert seq % MIX_ROWS == 0 and seq % ATTN_ROWS == 0 and MIX_ROWS % GMLP_CHUNK == 0
    assert MIX_ROWS % dec_seq == 0 and dec_batch % seg_per_tile == 0 and dec_seq >= HIST_ROWS
    assert dec_seq <= GMLP_CHUNK and PAST_LEN % GMLP_CHUNK == 0
    assert dec_batch % ATTN_SAMPLE_STREAMS == 0
    assert rows_p % FFN_ROWS == 0 and rows_s % FFN_ROWS == 0

    w = dict(norm_mix_g=norm_mix_g, w_in=w_in.astype(BF16), pool_w=pool_w.astype(BF16),
             pool_scale=pool_scale, gm_ln_g=gm_ln_g, gm_ln_b=gm_ln_b,
             w_mix_out=w_mix_out.astype(BF16), norm_xa_g=norm_xa_g,
             xa_wq=xa_wq.astype(BF16), xa_wo=xa_wo.astype(BF16))
    ffn_b = [a.astype(BF16) for a in (ffn_wg, ffn_wu, ffn_wd)]
    moe_b = [a.astype(BF16) for a in (moe_wg, moe_wu, moe_wd)]

    mk, mv = _memory_kv(mem_prompt.reshape(batch * n_mem, d), norm_mem_g,
                        xa_wk.astype(BF16), xa_wv.astype(BF16))
    new_mem_k = mk.reshape(depth, batch, n_mem, n_mem_heads, mem_hd)
    new_mem_v = mv.reshape(depth, batch, n_mem, n_mem_heads, mem_hd)
    pk, pv = mk.reshape(depth, batch, n_mem, d), mv.reshape(depth, batch, n_mem, d)

    hist_p0 = jnp.zeros((batch, HIST_ROWS, d_pool), F32)
    hist_s = jnp.pad(state_pool, ((0, 0), (0, 0), (HIST_ROWS - state_pool.shape[2], 0), (0, 0)))
    keep = HIST_ROWS - state_pool.shape[2]

    xp = x_prompt.reshape(rows_p, d)
    xs = x_sample.reshape(rows_s, d)
    pool_p, pool_s, v_rows = [], [], []
    for l in range(depth):
        wbig_p, bias_p = _gate_operands(gm_ws[l], gm_bs[l], MIX_ROWS, GMLP_CHUNK)
        wbig_s, bias_s = _gate_operands(gm_ws[l], gm_bs[l], MIX_ROWS, dec_seq)
        xp, hp = _mixer(xp, l > 0, batch, seq // MIX_ROWS, 1, MIX_ROWS, 0, hist_p0, l, w,
                        wbig_p, bias_p, False)
        xs, hs, vr = _mixer(xs, l > 0, dec_batch // seg_per_tile, 1, seg_per_tile, dec_seq,
                            PAST_LEN, hist_s[l], l, w, wbig_s, bias_s, True)
        pool_p.append(hp[:, keep:])
        pool_s.append(hs[:, keep:])
        v_rows.append(vr.reshape(dec_batch, dec_seq, d_gate))

        xp = _attention(xp, batch, seq // ATTN_ROWS, pk, pv, l, w, n_mem_heads)
        xs = _attention_cached(xs, ATTN_SAMPLE_STREAMS, cache_mem_k, cache_mem_v, l, w)

        final = l == depth - 1
        if l % 2 == 0:
            xp = _dense_ffn(xp, norm_ffn_g[l], *ffn_b, l // 2, norm_final_g, final)
            xs = _dense_ffn(xs, norm_ffn_g[l], *ffn_b, l // 2, norm_final_g, final)
        else:
            i = l // 2
            xp, xs = _moe_layer(xp, xs, norm_ffn_g[l], moe_router_w[i], moe_router_b[i], *moe_b, i,
                                norm_final_g, final)

    return (xp.reshape(batch, seq, d), xs.reshape(dec_batch, dec_seq, d),
            new_mem_k, new_mem_v, jnp.stack(pool_p), jnp.stack(pool_s), jnp.stack(v_rows))
```

```python
import functools

import jax
import jax.numpy as jnp
from jax import lax
from jax.experimental import pallas as pl
from jax.experimental.pallas import tpu as pltpu

EPS = 1e-6
PAST_LEN = 4096
POOL_WINDOWS = (2, 4, 8, 16)
HIST_ROWS = 16
GMLP_CHUNK = 128
LANES = 128
SUBLANES = 8
TOP_K = 2

V7X_VMEM_BYTES = 64 * 1024 * 1024
VMEM_LIMIT = V7X_VMEM_BYTES - 8 * 1024 * 1024

MIX_ROWS = 512
GATE_ROWS = 256
ATTN_ROWS = 1024
ATTN_SAMPLE_STREAMS = 8
FFN_ROWS = 512
FF_CHUNK = 256
MOVE_ROWS = 512

BF16 = jnp.bfloat16
F32 = jnp.float32


def _rmsnorm(x, g):
    return x * lax.rsqrt(jnp.mean(x * x, axis=-1, keepdims=True) + EPS) * g


def _dot(a, b):
    return jnp.dot(a, b, preferred_element_type=F32)


def _params(sem, vmem=VMEM_LIMIT):
    return pltpu.CompilerParams(dimension_semantics=sem, vmem_limit_bytes=vmem)


def _const_spec(shape, layer=None):
    nd = len(shape)
    if layer is None:
        return pl.BlockSpec(shape, lambda *_: (0,) * nd)
    return pl.BlockSpec((None,) + shape, lambda *_: (layer,) + (0,) * nd)


def _memkv_kernel(mem_ref, g_ref, wk_ref, wv_ref, k_ref, v_ref, k5_ref, v5_ref, sem):
    l = pl.program_id(0)
    m = _rmsnorm(mem_ref[...], g_ref[...]).astype(BF16)
    k_ref[0] = _dot(m, wk_ref[...])
    v_ref[0] = _dot(m, wv_ref[...])
    depth, n_b, n_mem, n_heads, hd = k5_ref.shape
    for layer in range(depth):
        @pl.when(l == layer)
        def _():
            copies = [pltpu.make_async_copy(src.at[0, pl.ds(b * n_mem, n_mem), pl.ds(h * hd, hd)],
                                            dst.at[layer, b, :, h, :], sem)
                      for src, dst in ((k_ref, k5_ref), (v_ref, v5_ref))
                      for b in range(n_b) for h in range(n_heads)]
            for c in copies:
                c.start()
            for c in copies:
                c.wait()


def _memory_kv(mem, norm_g, wk, wv, n_heads):
    depth, d = norm_g.shape
    n_b, n_mem, _ = mem.shape
    rows = n_b * n_mem
    lay = lambda l: (l, 0, 0)
    five_d = jax.ShapeDtypeStruct((depth, n_b, n_mem, n_heads, d // n_heads), F32)
    return pl.pallas_call(
        _memkv_kernel,
        grid=(depth,),
        in_specs=[pl.BlockSpec((rows, d), lambda l: (0, 0)),
                  pl.BlockSpec((None, 1, d), lay),
                  pl.BlockSpec((None, d, d), lay),
                  pl.BlockSpec((None, d, d), lay)],
        out_specs=[pl.BlockSpec((1, rows, d), lay)] * 2
                  + [pl.BlockSpec(memory_space=pl.ANY)] * 2,
        out_shape=[jax.ShapeDtypeStruct((depth, rows, d), F32)] * 2 + [five_d] * 2,
        scratch_shapes=[pltpu.SemaphoreType.DMA],
        compiler_params=_params(("arbitrary",)),
        name="memory_kv",
    )(mem.reshape(rows, d), norm_g.reshape(depth, 1, d), wk, wv)


def _mixer_kernel(x_ref, g_ref, win_ref, poolw_ref, pscale_ref, lng_ref, lnb_ref, wbig_ref,
                  bias_ref, wout_ref, hist_ref, o_ref, hist_out_ref, *rest, n_seg, seg_rows, pos0,
                  keep_v):
    v_ref = rest[0] if keep_v else None
    hist_sc = rest[-1]
    j = pl.program_id(1)
    d_pool = pscale_ref.shape[-1]
    d_gate = lng_ref.shape[-1]
    pg = d_pool // len(POOL_WINDOWS)
    n_heads = wbig_ref.shape[0]
    gh = d_gate // n_heads

    @pl.when(j == 0)
    def _():
        hist_sc[...] = hist_ref[...]

    x = x_ref[...]
    h = _rmsnorm(x, g_ref[...]).astype(BF16)
    z = _dot(h, win_ref[...])
    p = z[:, :d_pool]
    u = jax.nn.gelu(z[:, d_pool:d_pool + d_gate])
    vpre = jax.nn.gelu(z[:, d_pool + d_gate:])
    mu = jnp.mean(vpre, axis=-1, keepdims=True)
    vc = vpre - mu
    var = jnp.mean(vc * vc, axis=-1, keepdims=True)
    v = vc * lax.rsqrt(var + EPS) * lng_ref[...] + lnb_ref[...]
    if keep_v:
        v_ref[...] = v

    pos = pos0 + j * seg_rows + lax.broadcasted_iota(jnp.int32, (seg_rows, pg), 0)
    pd_segs = []
    for s in range(n_seg):
        ps = p[s * seg_rows:(s + 1) * seg_rows]
        ext = jnp.concatenate([hist_sc[s], ps], axis=0)
        hist_sc[s] = ps[seg_rows - HIST_ROWS:]
        cols = []
        for gi, w in enumerate(POOL_WINDOWS):
            acc = ext[:, gi * pg:(gi + 1) * pg]
            span = 1
            while span < w:
                acc = acc + pltpu.roll(acc, span, 0)
                span *= 2
            cnt = jnp.minimum(pos + 1, w).astype(F32)
            cols.append(acc[HIST_ROWS:] / cnt - ps[:, gi * pg:(gi + 1) * pg])
        pd_segs.append(jnp.concatenate(cols, axis=1))
    pd = pd_segs[0] if n_seg == 1 else jnp.concatenate(pd_segs, axis=0)
    hist_out_ref[...] = hist_sc[...]

    pdb = pd.astype(BF16)
    parts = []
    for gi in range(len(POOL_WINDOWS)):
        sl = slice(gi * pg, (gi + 1) * pg)
        parts.append(_dot(pdb[:, sl], poolw_ref[gi]) * pscale_ref[:, sl])
    vb = v.astype(BF16)
    gate_rows = wbig_ref.shape[1]
    for hi in range(n_heads):
        sl = slice(hi * gh, (hi + 1) * gh)
        mixed = [_dot(wbig_ref[hi], vb[r0:r0 + gate_rows, sl]) + bias_ref[:, sl]
                 for r0 in range(0, x.shape[0], gate_rows)]
        mixed = mixed[0] if len(mixed) == 1 else jnp.concatenate(mixed, axis=0)
        parts.append(u[:, sl] * mixed)
    cat = jnp.concatenate(parts, axis=1).astype(BF16)
    o_ref[...] = x + _dot(cat, wout_ref[...])


def _mixer(x, in_place, n_b, n_j, n_seg, seg_rows, pos0, hist, layer, w, wbig, bias, keep_v):
    d = x.shape[1]
    rows = n_seg * seg_rows
    d_in = w["w_in"].shape[-1]
    d_pool = w["pool_scale"].shape[-1]
    d_gate = w["gm_ln_g"].shape[-1]
    vec = lambda a: a[layer].reshape(1, -1)
    xspec = pl.BlockSpec((rows, d), lambda b, j: (b * n_j + j, 0))
    hspec = pl.BlockSpec((n_seg, HIST_ROWS, d_pool), lambda b, j: (b, 0, 0))
    in_specs = [xspec, _const_spec((1, d)), _const_spec((d, d_in), layer),
                _const_spec(w["pool_w"].shape[1:], layer),
                _const_spec((1, d_pool)), _const_spec((1, d_gate)), _const_spec((1, d_gate)),
                _const_spec(wbig.shape), _const_spec(bias.shape), _const_spec((d, d), layer),
                hspec]
    out_specs = [xspec, hspec]
    out_shape = [jax.ShapeDtypeStruct(x.shape, F32), jax.ShapeDtypeStruct(hist.shape, F32)]
    if keep_v:
        out_specs.append(pl.BlockSpec((rows, d_gate), lambda b, j: (b * n_j + j, 0)))
        out_shape.append(jax.ShapeDtypeStruct((x.shape[0], d_gate), F32))
    return pl.pallas_call(
        functools.partial(_mixer_kernel, n_seg=n_seg, seg_rows=seg_rows, pos0=pos0, keep_v=keep_v),
        grid=(n_b, n_j),
        in_specs=in_specs,
        out_specs=out_specs,
        out_shape=out_shape,
        scratch_shapes=[pltpu.VMEM((n_seg, HIST_ROWS, d_pool), F32)],
        input_output_aliases={0: 0} if in_place else {},
        compiler_params=_params(("arbitrary", "arbitrary")),
        name="token_mixer",
    )(x, vec(w["norm_mix_g"]), w["w_in"], w["pool_w"], vec(w["pool_scale"]),
      vec(w["gm_ln_g"]), vec(w["gm_ln_b"]), wbig, bias, w["w_mix_out"], hist)


def _gate_operands(ws, bs, rows, chunk):
    n_heads = ws.shape[0]
    gh = GMLP_CHUNK
    tri = jnp.tril(jnp.ones((chunk, chunk), dtype=bool))
    wc = jnp.where(tri[None], ws[:, :chunk, :chunk], 0)
    eye = jnp.eye(rows // chunk, dtype=ws.dtype)
    wbig = jnp.einsum("ab,hts->hatbs", eye, wc).reshape(n_heads, rows, rows).astype(BF16)
    bias = jnp.tile(bs[:, :chunk].T, (rows // chunk, 1))
    bias = jnp.repeat(bias, gh, axis=1)
    return wbig, bias


def _queries(x, g_ref, wq_ref, hd):
    h = _rmsnorm(x, g_ref[...]).astype(BF16)
    return (_dot(h, wq_ref[...]) * (hd ** -0.5)).astype(BF16)


def _attend(qs, k_head, v_head, n_heads, hd):
    outs = []
    for hi in range(n_heads):
        s = lax.dot_general(qs[:, hi * hd:(hi + 1) * hd], k_head(hi), (((1,), (1,)), ((), ())),
                            preferred_element_type=F32)
        e = jnp.exp(s - jnp.max(s, axis=-1, keepdims=True))
        prob = e * (1.0 / jnp.sum(e, axis=-1, keepdims=True))
        outs.append(_dot(prob.astype(BF16), v_head(hi)))
    return jnp.concatenate(outs, axis=1)


def _attn_kernel(x_ref, g_ref, wq_ref, wo_ref, k_ref, v_ref, o_ref, *, n_heads):
    x = x_ref[...]
    hd = x.shape[-1] // n_heads
    q = _queries(x, g_ref, wq_ref, hd)
    kb = k_ref[...].astype(BF16)
    vb = v_ref[...].astype(BF16)
    o = _attend(q, lambda hi: kb[:, hi * hd:(hi + 1) * hd], lambda hi: vb[:, hi * hd:(hi + 1) * hd],
                n_heads, hd)
    o_ref[...] = x + _dot(o.astype(BF16), wo_ref[...])


def _attention(x, n_b, n_j, mem_k, mem_v, layer, w, n_heads):
    d = x.shape[1]
    rows = x.shape[0] // (n_b * n_j)
    n_mem = mem_k.shape[2]
    xmap = lambda b, j: (b * n_j + j, 0)
    kvspec = pl.BlockSpec((None, None, n_mem, d), lambda b, j: (layer, b, 0, 0))
    return pl.pallas_call(
        functools.partial(_attn_kernel, n_heads=n_heads),
        grid=(n_b, n_j),
        in_specs=[pl.BlockSpec((rows, d), xmap), _const_spec((1, d)),
                  _const_spec((d, d), layer), _const_spec((d, d), layer), kvspec, kvspec],
        out_specs=pl.BlockSpec((rows, d), xmap),
        out_shape=jax.ShapeDtypeStruct(x.shape, F32),
        input_output_aliases={0: 0},
        compiler_params=_params(("arbitrary", "arbitrary")),
        name="cross_attention",
    )(x, w["norm_xa_g"][layer].reshape(1, d), w["xa_wq"], w["xa_wo"], mem_k, mem_v)


def _attn_cached_kernel(x_ref, g_ref, wq_ref, wo_ref, k_hbm, v_hbm, o_ref, kbuf, vbuf, sem, *,
                        layer, streams):
    i = pl.program_id(0)
    n_heads, hd = k_hbm.shape[-2:]
    slot = i % 2

    def copies(step, to_slot):
        src = pl.ds(step * streams, streams)
        return [pltpu.make_async_copy(hbm.at[layer, src, :, hi, :], buf.at[to_slot, hi],
                                      sem.at[to_slot])
                for hbm, buf in ((k_hbm, kbuf), (v_hbm, vbuf)) for hi in range(n_heads)]

    @pl.when(i == 0)
    def _():
        for c in copies(0, 0):
            c.start()

    @pl.when(i + 1 < pl.num_programs(0))
    def _():
        for c in copies(i + 1, 1 - slot):
            c.start()

    x = x_ref[...]
    q = _queries(x, g_ref, wq_ref, hd)
    for c in copies(i, slot):
        c.wait()
    seg = x.shape[0] // streams
    outs = []
    for hi in range(n_heads):
        qh = q[:, hi * hd:(hi + 1) * hd].reshape(streams, seg, hd)
        s = jnp.einsum("sld,smd->slm", qh, kbuf[slot, hi].astype(BF16),
                       preferred_element_type=F32)
        e = jnp.exp(s - jnp.max(s, axis=-1, keepdims=True))
        prob = e * (1.0 / jnp.sum(e, axis=-1, keepdims=True))
        oh = jnp.einsum("slm,smd->sld", prob.astype(BF16), vbuf[slot, hi].astype(BF16),
                        preferred_element_type=F32)
        outs.append(oh.reshape(streams * seg, hd))
    o = jnp.concatenate(outs, axis=1).astype(BF16)
    o_ref[...] = x + _dot(o, wo_ref[...])


def _attention_cached(x, streams, cache_k, cache_v, layer, w):
    d = x.shape[1]
    n_streams, n_mem, n_heads, hd = cache_k.shape[1:]
    rows = x.shape[0] // n_streams * streams
    xspec = pl.BlockSpec((rows, d), lambda i: (i, 0))
    hbm = pl.BlockSpec(memory_space=pl.ANY)
    buf = pltpu.VMEM((2, n_heads, streams, n_mem, hd), F32)
    return pl.pallas_call(
        functools.partial(_attn_cached_kernel, layer=layer, streams=streams),
        grid=(n_streams // streams,),
        in_specs=[xspec, _const_spec((1, d)), _const_spec((d, d), layer),
                  _const_spec((d, d), layer), hbm, hbm],
        out_specs=xspec,
        out_shape=jax.ShapeDtypeStruct(x.shape, F32),
        scratch_shapes=[buf, buf, pltpu.SemaphoreType.DMA((2,))],
        input_output_aliases={0: 0},
        compiler_params=_params(("arbitrary",)),
        name="cached_attention",
    )(x, w["norm_xa_g"][layer].reshape(1, d), w["xa_wq"], w["xa_wo"], cache_k, cache_v)


def _swiglu_into(o_ref, hb, wg_ref, wu_ref, wd_ref, base):
    d_ff = wg_ref.shape[-1]
    for c in range(d_ff // FF_CHUNK):
        sl = slice(c * FF_CHUNK, (c + 1) * FF_CHUNK)
        a = (jax.nn.silu(_dot(hb, wg_ref[:, sl])) * _dot(hb, wu_ref[:, sl])).astype(BF16)
        part = _dot(a, wd_ref[sl, :])
        if c == 0:
            o_ref[...] = part if base is None else base + part
        else:
            o_ref[...] += part


def _ffn_kernel(x_ref, g_ref, wg_ref, wu_ref, wd_ref, gf_ref, o_ref, *, final):
    x = x_ref[...]
    hb = _rmsnorm(x, g_ref[...]).astype(BF16)
    _swiglu_into(o_ref, hb, wg_ref, wu_ref, wd_ref, x)
    if final:
        o_ref[...] = _rmsnorm(o_ref[...], gf_ref[...])


def _dense_ffn(x, g, wg, wu, wd, idx, g_final, final):
    t, d = x.shape
    d_ff = wg.shape[-1]
    assert t % FFN_ROWS == 0 and d_ff % FF_CHUNK == 0
    xspec = pl.BlockSpec((FFN_ROWS, d), lambda i: (i, 0))
    return pl.pallas_call(
        functools.partial(_ffn_kernel, final=final),
        grid=(t // FFN_ROWS,),
        in_specs=[xspec, _const_spec((1, d)), _const_spec((d, d_ff), idx),
                  _const_spec((d, d_ff), idx), _const_spec((d_ff, d), idx), _const_spec((1, d))],
        out_specs=xspec,
        out_shape=jax.ShapeDtypeStruct((t, d), F32),
        input_output_aliases={0: 0},
        compiler_params=_params(("arbitrary",)),
        name="dense_ffn",
    )(x, g.reshape(1, d), wg, wu, wd, g_final.reshape(1, d))


def _two_source_specs(rows, d, n_a):
    return [pl.BlockSpec((rows, d), lambda i, *_: (jnp.minimum(i, n_a - 1), 0)),
            pl.BlockSpec((rows, d), lambda i, *_: (jnp.maximum(i - n_a, 0), 0))]


def _router_kernel(xa_ref, xb_ref, g_ref, rw_ref, rb_ref, before_ref, sel_ref, gate_ref, cnt_ref,
                   carry, *, n_a):
    i = pl.program_id(0)

    @pl.when(i == 0)
    def _():
        carry[...] = jnp.zeros_like(carry)

    x = jnp.where(i < n_a, xa_ref[...], xb_ref[...])
    h = _rmsnorm(x, g_ref[...])
    rw = rw_ref[...]
    h_hi, w_hi = h.astype(BF16), rw.astype(BF16)
    h_lo = (h - h_hi.astype(F32)).astype(BF16)
    w_lo = (rw - w_hi.astype(F32)).astype(BF16)
    logits = _dot(h_hi, w_hi) + (_dot(h_hi, w_lo) + _dot(h_lo, w_hi)) + rb_ref[...]
    lane_i = lax.broadcasted_iota(jnp.int32, logits.shape, 1)
    lane = lane_i.astype(F32)
    m0 = jnp.max(logits, axis=-1, keepdims=True)
    e0 = jnp.min(jnp.where(logits == m0, lane, float(LANES)), axis=-1, keepdims=True)
    rest = jnp.where(lane == e0, -jnp.inf, logits)
    m1 = jnp.max(rest, axis=-1, keepdims=True)
    e1 = jnp.min(jnp.where(rest == m1, lane, float(LANES)), axis=-1, keepdims=True)
    t = jnp.exp(m1 - m0)
    g0 = 1.0 / (1.0 + t)
    g1 = t * g0
    hot0 = (lane == e0).astype(F32)
    hot1 = (lane == e1).astype(F32)
    both = hot0 + hot1
    prior = _dot(before_ref[...], both.astype(BF16)) + carry[...]
    rank0 = jnp.sum(hot0 * prior, axis=-1, keepdims=True)
    rank1 = jnp.sum(hot1 * prior, axis=-1, keepdims=True)
    carry[...] += jnp.sum(both, axis=0, keepdims=True)
    cnt_ref[...] = carry[...]
    sel = jnp.where(lane_i == 0, e0, jnp.where(lane_i == 1, e1,
                    jnp.where(lane_i == 2, rank0, jnp.where(lane_i == 3, rank1, 0.0))))
    sel_ref[...] = jnp.transpose(sel)[:SUBLANES]
    gate_ref[...] = jnp.where(lane_i == 0, g0, jnp.where(lane_i == 1, g1, 0.0))


def _router(xa, xb, g, rw_pad, rb_pad):
    d = xa.shape[1]
    rows = FFN_ROWS
    n_a, n_b = xa.shape[0] // rows, xb.shape[0] // rows
    t = xa.shape[0] + xb.shape[0]
    before = jnp.tril(jnp.ones((rows, rows), F32), -1).astype(BF16)
    return pl.pallas_call(
        functools.partial(_router_kernel, n_a=n_a),
        grid=(n_a + n_b,),
        in_specs=_two_source_specs(rows, d, n_a) + [
            _const_spec((1, d)), _const_spec((d, LANES)), _const_spec((1, LANES)),
            _const_spec((rows, rows))],
        out_specs=[pl.BlockSpec((SUBLANES, rows), lambda i: (0, i)),
                   pl.BlockSpec((rows, LANES), lambda i: (i, 0)), _const_spec((1, LANES))],
        out_shape=[jax.ShapeDtypeStruct((SUBLANES, t), F32),
                   jax.ShapeDtypeStruct((t, LANES), F32),
                   jax.ShapeDtypeStruct((1, LANES), F32)],
        scratch_shapes=[pltpu.VMEM((1, LANES), F32)],
        compiler_params=_params(("arbitrary",)),
        name="moe_router",
    )(xa, xb, g.reshape(1, d), rw_pad, rb_pad, before)


def _dispatch_kernel(meta_ref, xa_ref, xb_ref, p0_ref, p1_ref, xs_ref, zero_sc, sem, zsem, *,
                     n_a, n_exp, min_tiles):
    i = pl.program_id(0)
    rows = xa_ref.shape[0]
    tile = zero_sc.shape[0]
    n_tiles = xs_ref.shape[0] // tile

    def zero_copy(row_end):
        start = pl.multiple_of(row_end - tile, tile)
        return pltpu.make_async_copy(zero_sc, xs_ref.at[pl.ds(start, tile)], zsem)

    fills = [(meta_ref[n_exp + e] > 0, meta_ref[e]) for e in range(n_exp)]
    fills += [(k >= meta_ref[2 * n_exp], (k + 1) * tile) for k in range(min_tiles, n_tiles)]

    @pl.when(i == 0)
    def _():
        zero_sc[...] = jnp.zeros_like(zero_sc)
        for needed, row_end in fills:
            @pl.when(needed)
            def _():
                zero_copy(row_end).start()
        for needed, row_end in fills:
            @pl.when(needed)
            def _():
                zero_copy(row_end).wait()

    def scatter_rows(x_ref):
        def start(grp, c):
            base = pl.multiple_of(grp * SUBLANES, SUBLANES)
            for u in range(SUBLANES):
                src = x_ref.at[pl.ds(base + u, 1)]
                pltpu.make_async_copy(src, xs_ref.at[pl.ds(p0_ref[base + u], 1)], sem).start()
                pltpu.make_async_copy(src, xs_ref.at[pl.ds(p1_ref[base + u], 1)], sem).start()
            return c

        lax.fori_loop(0, rows // SUBLANES, start, 0)
        for _ in range(TOP_K):
            pltpu.make_async_copy(x_ref, xs_ref.at[pl.ds(0, rows)], sem).wait()

    @pl.when(i < n_a)
    def _():
        scatter_rows(xa_ref)

    @pl.when(i >= n_a)
    def _():
        scatter_rows(xb_ref)


def _dispatch(xa, xb, pos0, pos1, meta, n_exp, n_sorted, tile):
    d = xa.shape[1]
    rows = MOVE_ROWS
    n_a, n_b = xa.shape[0] // rows, xb.shape[0] // rows
    t = xa.shape[0] + xb.shape[0]
    smem_rows = pl.BlockSpec((rows,), lambda i, *_: (i,), memory_space=pltpu.SMEM)
    min_tiles = -(-TOP_K * t // tile)
    return pl.pallas_call(
        functools.partial(_dispatch_kernel, n_a=n_a, n_exp=n_exp, min_tiles=min_tiles),
        grid_spec=pltpu.PrefetchScalarGridSpec(
            num_scalar_prefetch=1,
            grid=(n_a + n_b,),
            in_specs=_two_source_specs(rows, d, n_a) + [smem_rows, smem_rows],
            out_specs=pl.BlockSpec(memory_space=pl.ANY),
            scratch_shapes=[pltpu.VMEM((tile, d), F32), pltpu.SemaphoreType.DMA,
                            pltpu.SemaphoreType.DMA]),
        out_shape=jax.ShapeDtypeStruct((n_sorted, d), F32),
        compiler_params=_params(("arbitrary",)),
        name="moe_dispatch",
    )(meta, xa, xb, pos0, pos1)


def _expert_kernel(te_ref, nu_ref, xs_ref, g_ref, wg_ref, wu_ref, wd_ref, ys_ref):
    used = pl.program_id(0) < nu_ref[0]

    @pl.when(used)
    def _():
        hb = _rmsnorm(xs_ref[...], g_ref[...]).astype(BF16)
        _swiglu_into(ys_ref, hb, wg_ref, wu_ref, wd_ref, None)

    @pl.when(jnp.logical_not(used))
    def _():
        ys_ref[...] = jnp.zeros_like(ys_ref)


def _expert_ffn(xs, g, tile_expert, n_used, wg, wu, wd, idx, tile):
    n_sorted, d = xs.shape
    d_ff = wg.shape[-1]
    rowmap = lambda i, te, nu: (jnp.minimum(i, nu[0] - 1), 0)
    wmap = lambda i, te, nu: (idx, te[i], 0, 0)
    return pl.pallas_call(
        _expert_kernel,
        grid_spec=pltpu.PrefetchScalarGridSpec(
            num_scalar_prefetch=2,
            grid=(n_sorted // tile,),
            in_specs=[pl.BlockSpec((tile, d), rowmap),
                      pl.BlockSpec((1, d), lambda i, te, nu: (0, 0)),
                      pl.BlockSpec((None, None, d, d_ff), wmap),
                      pl.BlockSpec((None, None, d, d_ff), wmap),
                      pl.BlockSpec((None, None, d_ff, d), wmap)],
            out_specs=pl.BlockSpec((tile, d), lambda i, te, nu: (i, 0))),
        out_shape=jax.ShapeDtypeStruct((n_sorted, d), F32),
        compiler_params=_params(("arbitrary",)),
        name="expert_ffn",
    )(tile_expert, n_used, xs, g.reshape(1, d), wg, wu, wd)


def _combine_kernel(x_ref, p0_ref, p1_ref, p0n_ref, p1n_ref, gate_ref, gf_ref, ys_ref, o_ref, buf,
                    sem, *, final):
    i = pl.program_id(0)
    rows = x_ref.shape[0]
    slot = i % 2

    def gather(pa_ref, pb_ref, to_slot):
        def start(grp, c):
            base = pl.multiple_of(grp * SUBLANES, SUBLANES)
            for u in range(SUBLANES):
                r = base + u
                for k, p_ref in enumerate((pa_ref, pb_ref)):
                    pltpu.make_async_copy(ys_ref.at[pl.ds(p_ref[r], 1)],
                                          buf.at[to_slot, k, pl.ds(r, 1)], sem.at[to_slot]).start()
            return c

        lax.fori_loop(0, rows // SUBLANES, start, 0)

    @pl.when(i == 0)
    def _():
        gather(p0_ref, p1_ref, 0)

    @pl.when(i + 1 < pl.num_programs(0))
    def _():
        gather(p0n_ref, p1n_ref, 1 - slot)

    for k in range(TOP_K):
        pltpu.make_async_copy(ys_ref.at[pl.ds(0, rows)], buf.at[slot, k], sem.at[slot]).wait()
    gates = gate_ref[...]
    out = x_ref[...] + gates[:, 0:1] * buf[slot, 0] + gates[:, 1:2] * buf[slot, 1]
    if final:
        out = _rmsnorm(out, gf_ref[...])
    o_ref[...] = out


def _combine(x, off_rows, pos0, pos1, gates, ys, g_final, final):
    n_rows, d = x.shape
    rows = MOVE_ROWS
    off = off_rows // rows
    n = n_rows // rows
    smem_rows = pl.BlockSpec((rows,), lambda i: (off + i,), memory_space=pltpu.SMEM)
    smem_next = pl.BlockSpec((rows,), lambda i: (off + jnp.minimum(i + 1, n - 1),),
                             memory_space=pltpu.SMEM)
    xspec = pl.BlockSpec((rows, d), lambda i: (i, 0))
    return pl.pallas_call(
        functools.partial(_combine_kernel, final=final),
        grid=(n,),
        in_specs=[xspec, smem_rows, smem_rows, smem_next, smem_next,
                  pl.BlockSpec((rows, LANES), lambda i: (off + i, 0)), _const_spec((1, d)),
                  pl.BlockSpec(memory_space=pl.ANY)],
        out_specs=xspec,
        out_shape=jax.ShapeDtypeStruct((n_rows, d), F32),
        scratch_shapes=[pltpu.VMEM((2, TOP_K, rows, d), F32), pltpu.SemaphoreType.DMA((2,))],
        compiler_params=_params(("arbitrary",)),
        name="moe_combine",
    )(x, pos0, pos1, pos0, pos1, gates, g_final.reshape(1, d), ys)


def _moe_layer(xa, xb, g, rw, rb, wg, wu, wd, idx, g_final, final):
    d = xa.shape[1]
    t = xa.shape[0] + xb.shape[0]
    n_exp = rw.shape[-1]
    tile = FFN_ROWS
    rw_pad = jnp.zeros((d, LANES), F32).at[:, :n_exp].set(rw)
    rb_pad = jnp.full((1, LANES), -jnp.inf, F32).at[0, :n_exp].set(rb)
    sel, gates, counts = _router(xa, xb, g, rw_pad, rb_pad)

    counts = counts[0, :n_exp].astype(jnp.int32)
    gpad = (counts + tile - 1) // tile * tile
    gend = jnp.cumsum(gpad)
    gstart = gend - gpad
    sel = sel.astype(jnp.int32)
    pos0 = gstart[sel[0]] + sel[2]
    pos1 = gstart[sel[1]] + sel[3]
    n_tiles = (TOP_K * t + n_exp * (tile - 1)) // tile
    tile_ids = jnp.arange(n_tiles, dtype=jnp.int32)
    tile_expert = jnp.minimum(
        jnp.sum((gend // tile)[None, :] <= tile_ids[:, None], axis=1), n_exp - 1).astype(jnp.int32)
    n_used = (gend[-1:] // tile).astype(jnp.int32)

    meta = jnp.concatenate([gend, gpad, n_used]).astype(jnp.int32)
    xs = _dispatch(xa, xb, pos0, pos1, meta, n_exp, n_tiles * tile, tile)
    ys = _expert_ffn(xs, g, tile_expert, n_used, wg, wu, wd, idx, tile)
    return (_combine(xa, 0, pos0, pos1, gates, ys, g_final, final),
            _combine(xb, xa.shape[0], pos0, pos1, gates, ys, g_final, final))


def kernel(x_prompt, x_sample, cache_mem_k, cache_mem_v, state_pool, mem_prompt, norm_mix_g, w_in, pool_w, pool_scale, gm_ln_g, gm_ln_b, gm_ws, gm_bs, w_mix_out, norm_xa_g, norm_mem_g, xa_wq, xa_wk, xa_wv, xa_wo, norm_ffn_g, ffn_wg, ffn_wu, ffn_wd, moe_router_w, moe_router_b, moe_wg, moe_wu, moe_wd, norm_final_g):
    batch, seq, d = x_prompt.shape
    dec_batch, dec_seq, _ = x_sample.shape
    depth = norm_mix_g.shape[0]
    n_mem, n_mem_heads, mem_hd = cache_mem_k.shape[2:]
    d_pool = pool_scale.shape[-1]
    d_gate = gm_ln_g.shape[-1]
    rows_p, rows_s = batch * seq, dec_batch * dec_seq
    seg_per_tile = GATE_ROWS // dec_seq
    assert seq % MIX_ROWS == 0 and seq % ATTN_ROWS == 0 and MIX_ROWS % GATE_ROWS == 0
    assert GATE_ROWS % GMLP_CHUNK == 0
    assert GATE_ROWS % dec_seq == 0 and dec_batch % seg_per_tile == 0 and dec_seq >= HIST_ROWS
    assert dec_seq <= GMLP_CHUNK and PAST_LEN % GMLP_CHUNK == 0
    assert dec_batch % ATTN_SAMPLE_STREAMS == 0
    assert rows_p % FFN_ROWS == 0 and rows_s % FFN_ROWS == 0
    assert rows_p % MOVE_ROWS == 0 and rows_s % MOVE_ROWS == 0

    w = dict(norm_mix_g=norm_mix_g, w_in=w_in.astype(BF16), pool_w=pool_w.astype(BF16),
             pool_scale=pool_scale, gm_ln_g=gm_ln_g, gm_ln_b=gm_ln_b,
             w_mix_out=w_mix_out.astype(BF16), norm_xa_g=norm_xa_g,
             xa_wq=xa_wq.astype(BF16), xa_wo=xa_wo.astype(BF16))
    ffn_b = [a.astype(BF16) for a in (ffn_wg, ffn_wu, ffn_wd)]
    moe_b = [a.astype(BF16) for a in (moe_wg, moe_wu, moe_wd)]

    mk, mv, new_mem_k, new_mem_v = _memory_kv(mem_prompt, norm_mem_g, xa_wk.astype(BF16),
                                              xa_wv.astype(BF16), n_mem_heads)
    pk, pv = mk.reshape(depth, batch, n_mem, d), mv.reshape(depth, batch, n_mem, d)

    hist_p0 = jnp.zeros((batch, HIST_ROWS, d_pool), F32)
    hist_s = jnp.pad(state_pool, ((0, 0), (0, 0), (HIST_ROWS - state_pool.shape[2], 0), (0, 0)))
    keep = HIST_ROWS - state_pool.shape[2]

    xp = x_prompt.reshape(rows_p, d)
    xs = x_sample.reshape(rows_s, d)
    pool_p, pool_s, v_rows = [], [], []
    for l in range(depth):
        wbig_p, bias_p = _gate_operands(gm_ws[l], gm_bs[l], GATE_ROWS, GMLP_CHUNK)
        wbig_s, bias_s = _gate_operands(gm_ws[l], gm_bs[l], GATE_ROWS, dec_seq)
        xp, hp = _mixer(xp, l > 0, batch, seq // MIX_ROWS, 1, MIX_ROWS, 0, hist_p0, l, w,
                        wbig_p, bias_p, False)
        xs, hs, vr = _mixer(xs, l > 0, dec_batch // seg_per_tile, 1, seg_per_tile, dec_seq,
                            PAST_LEN, hist_s[l], l, w, wbig_s, bias_s, True)
        pool_p.append(hp[:, keep:])
        pool_s.append(hs[:, keep:])
        v_rows.append(vr.reshape(dec_batch, dec_seq, d_gate))

        xp = _attention(xp, batch, seq // ATTN_ROWS, pk, pv, l, w, n_mem_heads)
        xs = _attention_cached(xs, ATTN_SAMPLE_STREAMS, cache_mem_k, cache_mem_v, l, w)

        final = l == depth - 1
        if l % 2 == 0:
            xp = _dense_ffn(xp, norm_ffn_g[l], *ffn_b, l // 2, norm_final_g, final)
            xs = _dense_ffn(xs, norm_ffn_g[l], *ffn_b, l // 2, norm_final_g, final)
        else:
            i = l // 2
            xp, xs = _moe_layer(xp, xs, norm_ffn_g[l], moe_router_w[i], moe_router_b[i], *moe_b, i,
                                norm_final_g, final)

    return (xp.reshape(batch, seq, d), xs.reshape(dec_batch, dec_seq, d),
            new_mem_k, new_mem_v, jnp.stack(pool_p), jnp.stack(pool_s), jnp.stack(v_rows))
```

```python
import functools

import jax
import jax.numpy as jnp
from jax import lax
from jax.experimental import pallas as pl
from jax.experimental.pallas import tpu as pltpu

EPS = 1e-6
PAST_LEN = 4096
POOL_WINDOWS = (2, 4, 8, 16)
HIST_ROWS = 16
GMLP_CHUNK = 128
LANES = 128
SUBLANES = 8
TOP_K = 2

V7X_VMEM_BYTES = 64 * 1024 * 1024
VMEM_LIMIT = V7X_VMEM_BYTES - 8 * 1024 * 1024

MIX_ROWS = 512
GATE_ROWS = 256
ATTN_ROWS = 1024
ATTN_SAMPLE_STREAMS = 8
FFN_ROWS = 512
FF_CHUNK = 256
MOVE_ROWS = 512

BF16 = jnp.bfloat16
F32 = jnp.float32


def _rmsnorm(x, g):
    return x * lax.rsqrt(jnp.mean(x * x, axis=-1, keepdims=True) + EPS) * g


def _dot(a, b):
    return jnp.dot(a, b, preferred_element_type=F32)


def _params(sem, vmem=VMEM_LIMIT):
    return pltpu.CompilerParams(dimension_semantics=sem, vmem_limit_bytes=vmem)


def _const_spec(shape, layer=None, single_buffer=False):
    nd = len(shape)
    mode = dict(pipeline_mode=pl.Buffered(1)) if single_buffer else {}
    if layer is None:
        return pl.BlockSpec(shape, lambda *_: (0,) * nd, **mode)
    return pl.BlockSpec((None,) + shape, lambda *_: (layer,) + (0,) * nd, **mode)


def _side_cast(arrays, n_steps, step_of):
    specs, shapes = [], []
    for a in arrays:
        rows, cols = a.shape
        n_blk = n_steps
        while rows % (n_blk * 2 * SUBLANES):
            n_blk -= 1
        imap = lambda *ids, n_blk=n_blk: (jnp.minimum(step_of(*ids), n_blk - 1), 0)
        specs.append(pl.BlockSpec((rows // n_blk, cols), imap))
        shapes.append(jax.ShapeDtypeStruct(a.shape, BF16))
    return specs, list(specs), shapes


def _run_side_cast(in_refs, out_refs):
    for src, dst in zip(in_refs, out_refs):
        dst[...] = src[...].astype(BF16)


def _memkv_kernel(mem_ref, g_ref, wk_ref, wv_ref, k_ref, v_ref, k5_ref, v5_ref, sem):
    l = pl.program_id(0)
    m = _rmsnorm(mem_ref[...], g_ref[...]).astype(BF16)
    k_ref[0] = _dot(m, wk_ref[...])
    v_ref[0] = _dot(m, wv_ref[...])
    depth, n_b, n_mem, n_heads, hd = k5_ref.shape
    for layer in range(depth):
        @pl.when(l == layer)
        def _():
            copies = [pltpu.make_async_copy(src.at[0, pl.ds(b * n_mem, n_mem), pl.ds(h * hd, hd)],
                                            dst.at[layer, b, :, h, :], sem)
                      for src, dst in ((k_ref, k5_ref), (v_ref, v5_ref))
                      for b in range(n_b) for h in range(n_heads)]
            for c in copies:
                c.start()
            for c in copies:
                c.wait()


def _memory_kv(mem, norm_g, wk, wv, n_heads):
    depth, d = norm_g.shape
    n_b, n_mem, _ = mem.shape
    rows = n_b * n_mem
    lay = lambda l: (l, 0, 0)
    five_d = jax.ShapeDtypeStruct((depth, n_b, n_mem, n_heads, d // n_heads), F32)
    return pl.pallas_call(
        _memkv_kernel,
        grid=(depth,),
        in_specs=[pl.BlockSpec((rows, d), lambda l: (0, 0)),
                  pl.BlockSpec((None, 1, d), lay),
                  pl.BlockSpec((None, d, d), lay),
                  pl.BlockSpec((None, d, d), lay)],
        out_specs=[pl.BlockSpec((1, rows, d), lay)] * 2
                  + [pl.BlockSpec(memory_space=pl.ANY)] * 2,
        out_shape=[jax.ShapeDtypeStruct((depth, rows, d), F32)] * 2 + [five_d] * 2,
        scratch_shapes=[pltpu.SemaphoreType.DMA],
        compiler_params=_params(("arbitrary",)),
        name="memory_kv",
    )(mem.reshape(rows, d), norm_g.reshape(depth, 1, d), wk, wv)


def _mixer_kernel(x_ref, g_ref, win_ref, poolw_ref, pscale_ref, lng_ref, lnb_ref, wbig_ref,
                  bias_ref, wout_ref, hist_ref, *rest, n_seg, seg_rows, pos0, keep_v, n_cast):
    cast_in, rest = rest[:n_cast], rest[n_cast:]
    o_ref, hist_out_ref = rest[:2]
    v_ref = rest[2] if keep_v else None
    cast_out = rest[len(rest) - 1 - n_cast:-1]
    hist_sc = rest[-1]
    _run_side_cast(cast_in, cast_out)
    j = pl.program_id(1)
    d_pool = pscale_ref.shape[-1]
    d_gate = lng_ref.shape[-1]
    pg = d_pool // len(POOL_WINDOWS)
    n_heads = wbig_ref.shape[0]
    gh = d_gate // n_heads

    @pl.when(j == 0)
    def _():
        hist_sc[...] = hist_ref[...]

    x = x_ref[...]
    h = _rmsnorm(x, g_ref[...]).astype(BF16)
    z = _dot(h, win_ref[...])
    p = z[:, :d_pool]
    u = jax.nn.gelu(z[:, d_pool:d_pool + d_gate])
    vpre = jax.nn.gelu(z[:, d_pool + d_gate:])
    mu = jnp.mean(vpre, axis=-1, keepdims=True)
    vc = vpre - mu
    var = jnp.mean(vc * vc, axis=-1, keepdims=True)
    v = vc * lax.rsqrt(var + EPS) * lng_ref[...] + lnb_ref[...]
    if keep_v:
        v_ref[...] = v

    pos = pos0 + j * seg_rows + lax.broadcasted_iota(jnp.int32, (seg_rows, pg), 0)
    pd_segs = []
    for s in range(n_seg):
        ps = p[s * seg_rows:(s + 1) * seg_rows]
        ext = jnp.concatenate([hist_sc[s], ps], axis=0)
        hist_sc[s] = ps[seg_rows - HIST_ROWS:]
        cols = []
        for gi, w in enumerate(POOL_WINDOWS):
            acc = ext[:, gi * pg:(gi + 1) * pg]
            span = 1
            while span < w:
                acc = acc + pltpu.roll(acc, span, 0)
                span *= 2
            cnt = jnp.minimum(pos + 1, w).astype(F32)
            cols.append(acc[HIST_ROWS:] / cnt - ps[:, gi * pg:(gi + 1) * pg])
        pd_segs.append(jnp.concatenate(cols, axis=1))
    pd = pd_segs[0] if n_seg == 1 else jnp.concatenate(pd_segs, axis=0)
    hist_out_ref[...] = hist_sc[...]

    pdb = pd.astype(BF16)
    parts = []
    for gi in range(len(POOL_WINDOWS)):
        sl = slice(gi * pg, (gi + 1) * pg)
        parts.append(_dot(pdb[:, sl], poolw_ref[gi]) * pscale_ref[:, sl])
    vb = v.astype(BF16)
    gate_rows = wbig_ref.shape[1]
    for hi in range(n_heads):
        sl = slice(hi * gh, (hi + 1) * gh)
        mixed = [_dot(wbig_ref[hi], vb[r0:r0 + gate_rows, sl]) + bias_ref[:, sl]
                 for r0 in range(0, x.shape[0], gate_rows)]
        mixed = mixed[0] if len(mixed) == 1 else jnp.concatenate(mixed, axis=0)
        parts.append(u[:, sl] * mixed)
    cat = jnp.concatenate(parts, axis=1).astype(BF16)
    o_ref[...] = x + _dot(cat, wout_ref[...])


def _mixer(x, in_place, n_b, n_j, n_seg, seg_rows, pos0, hist, layer, w, wbig, bias, keep_v,
           cast=()):
    d = x.shape[1]
    rows = n_seg * seg_rows
    d_in = w["w_in"].shape[-1]
    d_pool = w["pool_scale"].shape[-1]
    d_gate = w["gm_ln_g"].shape[-1]
    vec = lambda a: a[layer].reshape(1, -1)
    xspec = pl.BlockSpec((rows, d), lambda b, j: (b * n_j + j, 0))
    hspec = pl.BlockSpec((n_seg, HIST_ROWS, d_pool), lambda b, j: (b, 0, 0))
    in_specs = [xspec, _const_spec((1, d)), _const_spec((d, d_in), layer),
                _const_spec(w["pool_w"].shape[1:], layer),
                _const_spec((1, d_pool)), _const_spec((1, d_gate)), _const_spec((1, d_gate)),
                _const_spec(wbig.shape), _const_spec(bias.shape), _const_spec((d, d), layer),
                hspec]
    out_specs = [xspec, hspec]
    out_shape = [jax.ShapeDtypeStruct(x.shape, F32), jax.ShapeDtypeStruct(hist.shape, F32)]
    if keep_v:
        out_specs.append(pl.BlockSpec((rows, d_gate), lambda b, j: (b * n_j + j, 0)))
        out_shape.append(jax.ShapeDtypeStruct((x.shape[0], d_gate), F32))
    c_in, c_out, c_shapes = _side_cast(cast, n_b * n_j, lambda b, j: b * n_j + j)
    return pl.pallas_call(
        functools.partial(_mixer_kernel, n_seg=n_seg, seg_rows=seg_rows, pos0=pos0, keep_v=keep_v,
                          n_cast=len(cast)),
        grid=(n_b, n_j),
        in_specs=in_specs + c_in,
        out_specs=out_specs + c_out,
        out_shape=out_shape + c_shapes,
        scratch_shapes=[pltpu.VMEM((n_seg, HIST_ROWS, d_pool), F32)],
        input_output_aliases={0: 0} if in_place else {},
        compiler_params=_params(("arbitrary", "arbitrary")),
        name="token_mixer",
    )(x, vec(w["norm_mix_g"]), w["w_in"], w["pool_w"], vec(w["pool_scale"]),
      vec(w["gm_ln_g"]), vec(w["gm_ln_b"]), wbig, bias, w["w_mix_out"], hist, *cast)


def _gate_operands(ws, bs, rows, chunk):
    n_heads = ws.shape[0]
    gh = GMLP_CHUNK
    tri = jnp.tril(jnp.ones((chunk, chunk), dtype=bool))
    wc = jnp.where(tri[None], ws[:, :chunk, :chunk], 0)
    eye = jnp.eye(rows // chunk, dtype=ws.dtype)
    wbig = jnp.einsum("ab,hts->hatbs", eye, wc).reshape(n_heads, rows, rows).astype(BF16)
    bias = jnp.tile(bs[:, :chunk].T, (rows // chunk, 1))
    bias = jnp.repeat(bias, gh, axis=1)
    return wbig, bias


def _queries(x, g_ref, wq_ref, hd):
    h = _rmsnorm(x, g_ref[...]).astype(BF16)
    return (_dot(h, wq_ref[...]) * (hd ** -0.5)).astype(BF16)


def _attend(qs, k_head, v_head, n_heads, hd):
    outs = []
    for hi in range(n_heads):
        s = lax.dot_general(qs[:, hi * hd:(hi + 1) * hd], k_head(hi), (((1,), (1,)), ((), ())),
                            preferred_element_type=F32)
        e = jnp.exp(s - jnp.max(s, axis=-1, keepdims=True))
        prob = e * (1.0 / jnp.sum(e, axis=-1, keepdims=True))
        outs.append(_dot(prob.astype(BF16), v_head(hi)))
    return jnp.concatenate(outs, axis=1)


def _attn_kernel(x_ref, g_ref, wq_ref, wo_ref, k_ref, v_ref, o_ref, *, n_heads):
    x = x_ref[...]
    hd = x.shape[-1] // n_heads
    q = _queries(x, g_ref, wq_ref, hd)
    kb = k_ref[...].astype(BF16)
    vb = v_ref[...].astype(BF16)
    o = _attend(q, lambda hi: kb[:, hi * hd:(hi + 1) * hd], lambda hi: vb[:, hi * hd:(hi + 1) * hd],
                n_heads, hd)
    o_ref[...] = x + _dot(o.astype(BF16), wo_ref[...])


def _attention(x, n_b, n_j, mem_k, mem_v, layer, w, n_heads):
    d = x.shape[1]
    rows = x.shape[0] // (n_b * n_j)
    n_mem = mem_k.shape[2]
    xmap = lambda b, j: (b * n_j + j, 0)
    kvspec = pl.BlockSpec((None, None, n_mem, d), lambda b, j: (layer, b, 0, 0))
    return pl.pallas_call(
        functools.partial(_attn_kernel, n_heads=n_heads),
        grid=(n_b, n_j),
        in_specs=[pl.BlockSpec((rows, d), xmap), _const_spec((1, d)),
                  _const_spec((d, d), layer), _const_spec((d, d), layer), kvspec, kvspec],
        out_specs=pl.BlockSpec((rows, d), xmap),
        out_shape=jax.ShapeDtypeStruct(x.shape, F32),
        input_output_aliases={0: 0},
        compiler_params=_params(("arbitrary", "arbitrary")),
        name="cross_attention",
    )(x, w["norm_xa_g"][layer].reshape(1, d), w["xa_wq"], w["xa_wo"], mem_k, mem_v)


def _attn_cached_kernel(x_ref, g_ref, wq_ref, wo_ref, k_hbm, v_hbm, o_ref, kbuf, vbuf, sem, *,
                        layer, streams):
    i = pl.program_id(0)
    n_heads, hd = k_hbm.shape[-2:]
    slot = i % 2

    def copies(step, to_slot):
        src = pl.ds(step * streams, streams)
        return [pltpu.make_async_copy(hbm.at[layer, src, :, hi, :], buf.at[to_slot, hi],
                                      sem.at[to_slot])
                for hbm, buf in ((k_hbm, kbuf), (v_hbm, vbuf)) for hi in range(n_heads)]

    @pl.when(i == 0)
    def _():
        for c in copies(0, 0):
            c.start()

    @pl.when(i + 1 < pl.num_programs(0))
    def _():
        for c in copies(i + 1, 1 - slot):
            c.start()

    x = x_ref[...]
    q = _queries(x, g_ref, wq_ref, hd)
    for c in copies(i, slot):
        c.wait()
    seg = x.shape[0] // streams
    outs = []
    for hi in range(n_heads):
        qh = q[:, hi * hd:(hi + 1) * hd].reshape(streams, seg, hd)
        s = jnp.einsum("sld,smd->slm", qh, kbuf[slot, hi].astype(BF16),
                       preferred_element_type=F32)
        e = jnp.exp(s - jnp.max(s, axis=-1, keepdims=True))
        prob = e * (1.0 / jnp.sum(e, axis=-1, keepdims=True))
        oh = jnp.einsum("slm,smd->sld", prob.astype(BF16), vbuf[slot, hi].astype(BF16),
                        preferred_element_type=F32)
        outs.append(oh.reshape(streams * seg, hd))
    o = jnp.concatenate(outs, axis=1).astype(BF16)
    o_ref[...] = x + _dot(o, wo_ref[...])


def _attention_cached(x, streams, cache_k, cache_v, layer, w):
    d = x.shape[1]
    n_streams, n_mem, n_heads, hd = cache_k.shape[1:]
    rows = x.shape[0] // n_streams * streams
    xspec = pl.BlockSpec((rows, d), lambda i: (i, 0))
    hbm = pl.BlockSpec(memory_space=pl.ANY)
    buf = pltpu.VMEM((2, n_heads, streams, n_mem, hd), F32)
    return pl.pallas_call(
        functools.partial(_attn_cached_kernel, layer=layer, streams=streams),
        grid=(n_streams // streams,),
        in_specs=[xspec, _const_spec((1, d)), _const_spec((d, d), layer),
                  _const_spec((d, d), layer), hbm, hbm],
        out_specs=xspec,
        out_shape=jax.ShapeDtypeStruct(x.shape, F32),
        scratch_shapes=[buf, buf, pltpu.SemaphoreType.DMA((2,))],
        input_output_aliases={0: 0},
        compiler_params=_params(("arbitrary",)),
        name="cached_attention",
    )(x, w["norm_xa_g"][layer].reshape(1, d), w["xa_wq"], w["xa_wo"], cache_k, cache_v)


def _swiglu_into(o_ref, hb, wg_ref, wu_ref, wd_ref, base):
    d_ff = wg_ref.shape[-1]
    for c in range(d_ff // FF_CHUNK):
        sl = slice(c * FF_CHUNK, (c + 1) * FF_CHUNK)
        a = (jax.nn.silu(_dot(hb, wg_ref[:, sl])) * _dot(hb, wu_ref[:, sl])).astype(BF16)
        part = _dot(a, wd_ref[sl, :])
        if c == 0:
            o_ref[...] = part if base is None else base + part
        else:
            o_ref[...] += part


def _ffn_kernel(x_ref, g_ref, wg_ref, wu_ref, wd_ref, gf_ref, *rest, final, n_cast):
    cast_in, o_ref, cast_out = rest[:n_cast], rest[n_cast], rest[n_cast + 1:]
    _run_side_cast(cast_in, cast_out)
    x = x_ref[...]
    hb = _rmsnorm(x, g_ref[...]).astype(BF16)
    _swiglu_into(o_ref, hb, wg_ref, wu_ref, wd_ref, x)
    if final:
        o_ref[...] = _rmsnorm(o_ref[...], gf_ref[...])


def _dense_ffn(x, g, wg, wu, wd, idx, g_final, final, cast=()):
    t, d = x.shape
    d_ff = wg.shape[-1]
    assert t % FFN_ROWS == 0 and d_ff % FF_CHUNK == 0
    n = t // FFN_ROWS
    xspec = pl.BlockSpec((FFN_ROWS, d), lambda i: (i, 0))
    c_in, c_out, c_shapes = _side_cast(cast, n, lambda i: i)
    wspec = lambda shape: _const_spec(shape, idx, single_buffer=True)
    return pl.pallas_call(
        functools.partial(_ffn_kernel, final=final, n_cast=len(cast)),
        grid=(n,),
        in_specs=[xspec, _const_spec((1, d)), wspec((d, d_ff)), wspec((d, d_ff)),
                  wspec((d_ff, d)), _const_spec((1, d))] + c_in,
        out_specs=[xspec] + c_out,
        out_shape=[jax.ShapeDtypeStruct((t, d), F32)] + c_shapes,
        input_output_aliases={0: 0},
        compiler_params=_params(("arbitrary",)),
        name="dense_ffn",
    )(x, g.reshape(1, d), wg, wu, wd, g_final.reshape(1, d), *cast)


def _two_source_specs(rows, d, n_a):
    return [pl.BlockSpec((rows, d), lambda i, *_: (jnp.minimum(i, n_a - 1), 0)),
            pl.BlockSpec((rows, d), lambda i, *_: (jnp.maximum(i - n_a, 0), 0))]


def _router_kernel(xa_ref, xb_ref, g_ref, rw_ref, rb_ref, before_ref, sel_ref, gate_ref, cnt_ref,
                   carry, *, n_a):
    i = pl.program_id(0)

    @pl.when(i == 0)
    def _():
        carry[...] = jnp.zeros_like(carry)

    x = jnp.where(i < n_a, xa_ref[...], xb_ref[...])
    h = _rmsnorm(x, g_ref[...])
    rw = rw_ref[...]
    h_hi, w_hi = h.astype(BF16), rw.astype(BF16)
    h_lo = (h - h_hi.astype(F32)).astype(BF16)
    w_lo = (rw - w_hi.astype(F32)).astype(BF16)
    logits = _dot(h_hi, w_hi) + (_dot(h_hi, w_lo) + _dot(h_lo, w_hi)) + rb_ref[...]
    lane_i = lax.broadcasted_iota(jnp.int32, logits.shape, 1)
    lane = lane_i.astype(F32)
    m0 = jnp.max(logits, axis=-1, keepdims=True)
    e0 = jnp.min(jnp.where(logits == m0, lane, float(LANES)), axis=-1, keepdims=True)
    rest = jnp.where(lane == e0, -jnp.inf, logits)
    m1 = jnp.max(rest, axis=-1, keepdims=True)
    e1 = jnp.min(jnp.where(rest == m1, lane, float(LANES)), axis=-1, keepdims=True)
    t = jnp.exp(m1 - m0)
    g0 = 1.0 / (1.0 + t)
    g1 = t * g0
    hot0 = (lane == e0).astype(F32)
    hot1 = (lane == e1).astype(F32)
    both = hot0 + hot1
    prior = _dot(before_ref[...], both.astype(BF16)) + carry[...]
    rank0 = jnp.sum(hot0 * prior, axis=-1, keepdims=True)
    rank1 = jnp.sum(hot1 * prior, axis=-1, keepdims=True)
    carry[...] += jnp.sum(both, axis=0, keepdims=True)
    cnt_ref[...] = carry[...]
    sel = jnp.where(lane_i == 0, e0, jnp.where(lane_i == 1, e1,
                    jnp.where(lane_i == 2, rank0, jnp.where(lane_i == 3, rank1, 0.0))))
    sel_ref[...] = jnp.transpose(sel)[:SUBLANES]
    gate_ref[...] = jnp.where(lane_i == 0, g0, jnp.where(lane_i == 1, g1, 0.0))


def _router(xa, xb, g, rw_pad, rb_pad):
    d = xa.shape[1]
    rows = FFN_ROWS
    n_a, n_b = xa.shape[0] // rows, xb.shape[0] // rows
    t = xa.shape[0] + xb.shape[0]
    before = jnp.tril(jnp.ones((rows, rows), F32), -1).astype(BF16)
    return pl.pallas_call(
        functools.partial(_router_kernel, n_a=n_a),
        grid=(n_a + n_b,),
        in_specs=_two_source_specs(rows, d, n_a) + [
            _const_spec((1, d)), _const_spec((d, LANES)), _const_spec((1, LANES)),
            _const_spec((rows, rows))],
        out_specs=[pl.BlockSpec((SUBLANES, rows), lambda i: (0, i)),
                   pl.BlockSpec((rows, LANES), lambda i: (i, 0)), _const_spec((1, LANES))],
        out_shape=[jax.ShapeDtypeStruct((SUBLANES, t), F32),
                   jax.ShapeDtypeStruct((t, LANES), F32),
                   jax.ShapeDtypeStruct((1, LANES), F32)],
        scratch_shapes=[pltpu.VMEM((1, LANES), F32)],
        compiler_params=_params(("arbitrary",)),
        name="moe_router",
    )(xa, xb, g.reshape(1, d), rw_pad, rb_pad, before)


def _dispatch_kernel(meta_ref, xa_ref, xb_ref, p0_ref, p1_ref, xs_ref, zero_sc, sem, zsem, *,
                     n_a, n_exp, min_tiles):
    i = pl.program_id(0)
    rows = xa_ref.shape[0]
    tile = zero_sc.shape[0]
    n_tiles = xs_ref.shape[0] // tile

    def zero_copy(row_end):
        start = pl.multiple_of(row_end - tile, tile)
        return pltpu.make_async_copy(zero_sc, xs_ref.at[pl.ds(start, tile)], zsem)

    fills = [(meta_ref[n_exp + e] > 0, meta_ref[e]) for e in range(n_exp)]
    fills += [(k >= meta_ref[2 * n_exp], (k + 1) * tile) for k in range(min_tiles, n_tiles)]

    @pl.when(i == 0)
    def _():
        zero_sc[...] = jnp.zeros_like(zero_sc)
        for needed, row_end in fills:
            @pl.when(needed)
            def _():
                zero_copy(row_end).start()
        for needed, row_end in fills:
            @pl.when(needed)
            def _():
                zero_copy(row_end).wait()

    def scatter_rows(x_ref):
        def start(grp, c):
            base = pl.multiple_of(grp * SUBLANES, SUBLANES)
            for u in range(SUBLANES):
                src = x_ref.at[pl.ds(base + u, 1)]
                pltpu.make_async_copy(src, xs_ref.at[pl.ds(p0_ref[base + u], 1)], sem).start(0)
                pltpu.make_async_copy(src, xs_ref.at[pl.ds(p1_ref[base + u], 1)], sem).start(1)
            return c

        lax.fori_loop(0, rows // SUBLANES, start, 0)
        for _ in range(TOP_K):
            pltpu.make_async_copy(x_ref, xs_ref.at[pl.ds(0, rows)], sem).wait()

    @pl.when(i < n_a)
    def _():
        scatter_rows(xa_ref)

    @pl.when(i >= n_a)
    def _():
        scatter_rows(xb_ref)


def _dispatch(xa, xb, pos0, pos1, meta, n_exp, n_sorted, tile):
    d = xa.shape[1]
    rows = MOVE_ROWS
    n_a, n_b = xa.shape[0] // rows, xb.shape[0] // rows
    t = xa.shape[0] + xb.shape[0]
    smem_rows = pl.BlockSpec((rows,), lambda i, *_: (i,), memory_space=pltpu.SMEM)
    min_tiles = -(-TOP_K * t // tile)
    return pl.pallas_call(
        functools.partial(_dispatch_kernel, n_a=n_a, n_exp=n_exp, min_tiles=min_tiles),
        grid_spec=pltpu.PrefetchScalarGridSpec(
            num_scalar_prefetch=1,
            grid=(n_a + n_b,),
            in_specs=_two_source_specs(rows, d, n_a) + [smem_rows, smem_rows],
            out_specs=pl.BlockSpec(memory_space=pl.ANY),
            scratch_shapes=[pltpu.VMEM((tile, d), F32), pltpu.SemaphoreType.DMA,
                            pltpu.SemaphoreType.DMA]),
        out_shape=jax.ShapeDtypeStruct((n_sorted, d), F32),
        compiler_params=_params(("arbitrary",)),
        name="moe_dispatch",
    )(meta, xa, xb, pos0, pos1)


def _expert_kernel(te_ref, nu_ref, xs_ref, g_ref, wg_ref, wu_ref, wd_ref, ys_ref):
    used = pl.program_id(0) < nu_ref[0]

    @pl.when(used)
    def _():
        hb = _rmsnorm(xs_ref[...], g_ref[...]).astype(BF16)
        _swiglu_into(ys_ref, hb, wg_ref, wu_ref, wd_ref, None)

    @pl.when(jnp.logical_not(used))
    def _():
        ys_ref[...] = jnp.zeros_like(ys_ref)


def _expert_ffn(xs, g, tile_expert, n_used, wg, wu, wd, tile):
    n_sorted, d = xs.shape
    d_ff = wg.shape[-1]
    rowmap = lambda i, te, nu: (jnp.minimum(i, nu[0] - 1), 0)
    wmap = lambda i, te, nu: (te[i], 0, 0)
    return pl.pallas_call(
        _expert_kernel,
        grid_spec=pltpu.PrefetchScalarGridSpec(
            num_scalar_prefetch=2,
            grid=(n_sorted // tile,),
            in_specs=[pl.BlockSpec((tile, d), rowmap),
                      pl.BlockSpec((1, d), lambda i, te, nu: (0, 0)),
                      pl.BlockSpec((None, d, d_ff), wmap),
                      pl.BlockSpec((None, d, d_ff), wmap),
                      pl.BlockSpec((None, d_ff, d), wmap)],
            out_specs=pl.BlockSpec((tile, d), lambda i, te, nu: (i, 0))),
        out_shape=jax.ShapeDtypeStruct((n_sorted, d), F32),
        compiler_params=_params(("arbitrary",)),
        name="expert_ffn",
    )(tile_expert, n_used, xs, g.reshape(1, d), wg, wu, wd)


def _combine_kernel(x_ref, p0_ref, p1_ref, p0n_ref, p1n_ref, gate_ref, gf_ref, ys_ref, o_ref, buf,
                    sem, *, final):
    i = pl.program_id(0)
    rows = x_ref.shape[0]
    slot = i % 2

    def gather(pa_ref, pb_ref, to_slot):
        def start(grp, c):
            base = pl.multiple_of(grp * SUBLANES, SUBLANES)
            for u in range(SUBLANES):
                r = base + u
                for k, p_ref in enumerate((pa_ref, pb_ref)):
                    pltpu.make_async_copy(ys_ref.at[pl.ds(p_ref[r], 1)],
                                          buf.at[to_slot, k, pl.ds(r, 1)], sem.at[to_slot]).start(k)
            return c

        lax.fori_loop(0, rows // SUBLANES, start, 0)

    @pl.when(i == 0)
    def _():
        gather(p0_ref, p1_ref, 0)

    @pl.when(i + 1 < pl.num_programs(0))
    def _():
        gather(p0n_ref, p1n_ref, 1 - slot)

    for k in range(TOP_K):
        pltpu.make_async_copy(ys_ref.at[pl.ds(0, rows)], buf.at[slot, k], sem.at[slot]).wait()
    gates = gate_ref[...]
    out = x_ref[...] + gates[:, 0:1] * buf[slot, 0] + gates[:, 1:2] * buf[slot, 1]
    if final:
        out = _rmsnorm(out, gf_ref[...])
    o_ref[...] = out


def _combine(x, off_rows, pos0, pos1, gates, ys, g_final, final):
    n_rows, d = x.shape
    rows = MOVE_ROWS
    off = off_rows // rows
    n = n_rows // rows
    smem_rows = pl.BlockSpec((rows,), lambda i: (off + i,), memory_space=pltpu.SMEM)
    smem_next = pl.BlockSpec((rows,), lambda i: (off + jnp.minimum(i + 1, n - 1),),
                             memory_space=pltpu.SMEM)
    xspec = pl.BlockSpec((rows, d), lambda i: (i, 0))
    return pl.pallas_call(
        functools.partial(_combine_kernel, final=final),
        grid=(n,),
        in_specs=[xspec, smem_rows, smem_rows, smem_next, smem_next,
                  pl.BlockSpec((rows, LANES), lambda i: (off + i, 0)), _const_spec((1, d)),
                  pl.BlockSpec(memory_space=pl.ANY)],
        out_specs=xspec,
        out_shape=jax.ShapeDtypeStruct((n_rows, d), F32),
        scratch_shapes=[pltpu.VMEM((2, TOP_K, rows, d), F32), pltpu.SemaphoreType.DMA((2,))],
        compiler_params=_params(("arbitrary",)),
        name="moe_combine",
    )(x, pos0, pos1, pos0, pos1, gates, g_final.reshape(1, d), ys)


def _moe_layer(xa, xb, g, rw, rb, wg, wu, wd, g_final, final):
    d = xa.shape[1]
    t = xa.shape[0] + xb.shape[0]
    n_exp = rw.shape[-1]
    tile = FFN_ROWS
    rw_pad = jnp.zeros((d, LANES), F32).at[:, :n_exp].set(rw)
    rb_pad = jnp.full((1, LANES), -jnp.inf, F32).at[0, :n_exp].set(rb)
    sel, gates, counts = _router(xa, xb, g, rw_pad, rb_pad)

    counts = counts[0, :n_exp].astype(jnp.int32)
    gpad = (counts + tile - 1) // tile * tile
    gend = jnp.cumsum(gpad)
    gstart = gend - gpad
    sel = sel.astype(jnp.int32)
    pos0 = gstart[sel[0]] + sel[2]
    pos1 = gstart[sel[1]] + sel[3]
    n_tiles = (TOP_K * t + n_exp * (tile - 1)) // tile
    tile_ids = jnp.arange(n_tiles, dtype=jnp.int32)
    tile_expert = jnp.minimum(
        jnp.sum((gend // tile)[None, :] <= tile_ids[:, None], axis=1), n_exp - 1).astype(jnp.int32)
    n_used = (gend[-1:] // tile).astype(jnp.int32)

    meta = jnp.concatenate([gend, gpad, n_used]).astype(jnp.int32)
    xs = _dispatch(xa, xb, pos0, pos1, meta, n_exp, n_tiles * tile, tile)
    ys = _expert_ffn(xs, g, tile_expert, n_used, wg, wu, wd, tile)
    return (_combine(xa, 0, pos0, pos1, gates, ys, g_final, final),
            _combine(xb, xa.shape[0], pos0, pos1, gates, ys, g_final, final))


def kernel(x_prompt, x_sample, cache_mem_k, cache_mem_v, state_pool, mem_prompt, norm_mix_g, w_in, pool_w, pool_scale, gm_ln_g, gm_ln_b, gm_ws, gm_bs, w_mix_out, norm_xa_g, norm_mem_g, xa_wq, xa_wk, xa_wv, xa_wo, norm_ffn_g, ffn_wg, ffn_wu, ffn_wd, moe_router_w, moe_router_b, moe_wg, moe_wu, moe_wd, norm_final_g):
    batch, seq, d = x_prompt.shape
    dec_batch, dec_seq, _ = x_sample.shape
    depth = norm_mix_g.shape[0]
    n_mem, n_mem_heads, mem_hd = cache_mem_k.shape[2:]
    d_pool = pool_scale.shape[-1]
    d_gate = gm_ln_g.shape[-1]
    rows_p, rows_s = batch * seq, dec_batch * dec_seq
    seg_per_tile = GATE_ROWS // dec_seq
    assert seq % MIX_ROWS == 0 and seq % ATTN_ROWS == 0 and MIX_ROWS % GATE_ROWS == 0
    assert GATE_ROWS % GMLP_CHUNK == 0
    assert GATE_ROWS % dec_seq == 0 and dec_batch % seg_per_tile == 0 and dec_seq >= HIST_ROWS
    assert dec_seq <= GMLP_CHUNK and PAST_LEN % GMLP_CHUNK == 0
    assert dec_batch % ATTN_SAMPLE_STREAMS == 0
    assert rows_p % FFN_ROWS == 0 and rows_s % FFN_ROWS == 0
    assert rows_p % MOVE_ROWS == 0 and rows_s % MOVE_ROWS == 0

    w = dict(norm_mix_g=norm_mix_g, w_in=w_in.astype(BF16), pool_w=pool_w.astype(BF16),
             pool_scale=pool_scale, gm_ln_g=gm_ln_g, gm_ln_b=gm_ln_b,
             w_mix_out=w_mix_out.astype(BF16), norm_xa_g=norm_xa_g,
             xa_wq=xa_wq.astype(BF16), xa_wo=xa_wo.astype(BF16))
    ffn_b = [a.astype(BF16) for a in (ffn_wg, ffn_wu, ffn_wd)]

    mk, mv, new_mem_k, new_mem_v = _memory_kv(mem_prompt, norm_mem_g, xa_wk.astype(BF16),
                                              xa_wv.astype(BF16), n_mem_heads)
    pk, pv = mk.reshape(depth, batch, n_mem, d), mv.reshape(depth, batch, n_mem, d)

    hist_p0 = jnp.zeros((batch, HIST_ROWS, d_pool), F32)
    hist_s = jnp.pad(state_pool, ((0, 0), (0, 0), (HIST_ROWS - state_pool.shape[2], 0), (0, 0)))
    keep = HIST_ROWS - state_pool.shape[2]

    xp = x_prompt.reshape(rows_p, d)
    xs = x_sample.reshape(rows_s, d)
    pool_p, pool_s, v_rows = [], [], []
    n_exp, _, d_ff = moe_wg.shape[1:]
    expert_w = {}
    for l in range(depth):
        wbig_p, bias_p = _gate_operands(gm_ws[l], gm_bs[l], GATE_ROWS, GMLP_CHUNK)
        wbig_s, bias_s = _gate_operands(gm_ws[l], gm_bs[l], GATE_ROWS, dec_seq)
        mix_cast = [moe_wd[l // 2].reshape(n_exp * d_ff, d)] if l % 2 == 1 else []
        xp, hp, *mix_done = _mixer(xp, l > 0, batch, seq // MIX_ROWS, 1, MIX_ROWS, 0, hist_p0, l,
                                   w, wbig_p, bias_p, False, mix_cast)
        if mix_done:
            expert_w["wd"] = mix_done[0].reshape(n_exp, d_ff, d)
        xs, hs, vr = _mixer(xs, l > 0, dec_batch // seg_per_tile, 1, seg_per_tile, dec_seq,
                            PAST_LEN, hist_s[l], l, w, wbig_s, bias_s, True)
        pool_p.append(hp[:, keep:])
        pool_s.append(hs[:, keep:])
        v_rows.append(vr.reshape(dec_batch, dec_seq, d_gate))

        xp = _attention(xp, batch, seq // ATTN_ROWS, pk, pv, l, w, n_mem_heads)
        xs = _attention_cached(xs, ATTN_SAMPLE_STREAMS, cache_mem_k, cache_mem_v, l, w)

        final = l == depth - 1
        if l % 2 == 0:
            ffn_cast = ([moe_wg[l // 2].reshape(n_exp * d, d_ff),
                         moe_wu[l // 2].reshape(n_exp * d, d_ff)] if l + 1 < depth else [])
            xp, *ffn_done = _dense_ffn(xp, norm_ffn_g[l], *ffn_b, l // 2, norm_final_g, final,
                                       ffn_cast)
            xs, = _dense_ffn(xs, norm_ffn_g[l], *ffn_b, l // 2, norm_final_g, final)
            if ffn_done:
                expert_w["wg"], expert_w["wu"] = (a.reshape(n_exp, d, d_ff) for a in ffn_done)
        else:
            i = l // 2
            xp, xs = _moe_layer(xp, xs, norm_ffn_g[l], moe_router_w[i], moe_router_b[i],
                                expert_w["wg"], expert_w["wu"], expert_w["wd"], norm_final_g, final)

    return (xp.reshape(batch, seq, d), xs.reshape(dec_batch, dec_seq, d),
            new_mem_k, new_mem_v, jnp.stack(pool_p), jnp.stack(pool_s), jnp.stack(v_rows))
```

```python
import functools

import jax
import jax.numpy as jnp
from jax import lax
from jax.experimental import pallas as pl
from jax.experimental.pallas import tpu as pltpu

EPS = 1e-6
PAST_LEN = 4096
POOL_WINDOWS = (2, 4, 8, 16)
HIST_ROWS = 16
GMLP_CHUNK = 128
LANES = 128
SUBLANES = 8
TOP_K = 2

V7X_VMEM_BYTES = 64 * 1024 * 1024
VMEM_LIMIT = V7X_VMEM_BYTES - 8 * 1024 * 1024

MIX_ROWS = 1024
GATE_ROWS = 256
ATTN_SAMPLE_STREAMS = 8
FFN_ROWS = 512
FF_CHUNK = 256
MOVE_ROWS = 512

BF16 = jnp.bfloat16
F32 = jnp.float32


def _rmsnorm(x, g):
    return x * lax.rsqrt(jnp.mean(x * x, axis=-1, keepdims=True) + EPS) * g


def _dot(a, b):
    return jnp.dot(a, b, preferred_element_type=F32)


def _params(sem, vmem=VMEM_LIMIT):
    return pltpu.CompilerParams(dimension_semantics=sem, vmem_limit_bytes=vmem)


def _const_spec(shape, layer=None, single_buffer=False):
    nd = len(shape)
    mode = dict(pipeline_mode=pl.Buffered(1)) if single_buffer else {}
    if layer is None:
        return pl.BlockSpec(shape, lambda *_: (0,) * nd, **mode)
    return pl.BlockSpec((None,) + shape, lambda *_: (layer,) + (0,) * nd, **mode)


def _side_cast(arrays, n_steps, step_of):
    specs, shapes = [], []
    for a in arrays:
        rows, cols = a.shape
        n_blk = n_steps
        while rows % (n_blk * 2 * SUBLANES):
            n_blk -= 1
        imap = lambda *ids, n_blk=n_blk: (jnp.minimum(step_of(*ids), n_blk - 1), 0)
        specs.append(pl.BlockSpec((rows // n_blk, cols), imap))
        shapes.append(jax.ShapeDtypeStruct(a.shape, BF16))
    return specs, list(specs), shapes


def _run_side_cast(in_refs, out_refs):
    for src, dst in zip(in_refs, out_refs):
        dst[...] = src[...].astype(BF16)


def _memkv_kernel(mem_ref, g_ref, wk_ref, wv_ref, k_ref, v_ref, k5_ref, v5_ref, sem):
    l = pl.program_id(0)
    m = _rmsnorm(mem_ref[...], g_ref[...]).astype(BF16)
    k_ref[0] = _dot(m, wk_ref[...].astype(BF16))
    v_ref[0] = _dot(m, wv_ref[...].astype(BF16))
    depth, n_b, n_mem, n_heads, hd = k5_ref.shape
    for layer in range(depth):
        @pl.when(l == layer)
        def _():
            copies = [pltpu.make_async_copy(src.at[0, pl.ds(b * n_mem, n_mem), pl.ds(h * hd, hd)],
                                            dst.at[layer, b, :, h, :], sem)
                      for src, dst in ((k_ref, k5_ref), (v_ref, v5_ref))
                      for b in range(n_b) for h in range(n_heads)]
            for c in copies:
                c.start()
            for c in copies:
                c.wait()


def _memory_kv(mem, norm_g, wk, wv, n_heads):
    depth, d = norm_g.shape
    n_b, n_mem, _ = mem.shape
    rows = n_b * n_mem
    lay = lambda l: (l, 0, 0)
    five_d = jax.ShapeDtypeStruct((depth, n_b, n_mem, n_heads, d // n_heads), F32)
    return pl.pallas_call(
        _memkv_kernel,
        grid=(depth,),
        in_specs=[pl.BlockSpec((rows, d), lambda l: (0, 0)),
                  pl.BlockSpec((None, 1, d), lay),
                  pl.BlockSpec((None, d, d), lay),
                  pl.BlockSpec((None, d, d), lay)],
        out_specs=[pl.BlockSpec((1, rows, d), lay)] * 2
                  + [pl.BlockSpec(memory_space=pl.ANY)] * 2,
        out_shape=[jax.ShapeDtypeStruct((depth, rows, d), F32)] * 2 + [five_d] * 2,
        scratch_shapes=[pltpu.SemaphoreType.DMA],
        compiler_params=_params(("arbitrary",)),
        name="memory_kv",
    )(mem.reshape(rows, d), norm_g.reshape(depth, 1, d), wk, wv)


def _mixer_kernel(x_ref, g_ref, win_ref, poolw_ref, pscale_ref, lng_ref, lnb_ref, wbig_ref,
                  bias_ref, wout_ref, hist_ref, *rest, n_seg, seg_rows, pos0, keep_v, n_cast,
                  attn_heads):
    attn_refs, rest = (rest[:5], rest[5:]) if attn_heads else ((), rest)
    cast_in, rest = rest[:n_cast], rest[n_cast:]
    o_ref, hist_out_ref = rest[:2]
    v_ref = rest[2] if keep_v else None
    cast_out = rest[len(rest) - 1 - n_cast:-1]
    hist_sc = rest[-1]
    _run_side_cast(cast_in, cast_out)
    j = pl.program_id(1)
    d_pool = pscale_ref.shape[-1]
    d_gate = lng_ref.shape[-1]
    pg = d_pool // len(POOL_WINDOWS)
    n_heads = wbig_ref.shape[0]
    gh = d_gate // n_heads

    @pl.when(j == 0)
    def _():
        hist_sc[...] = hist_ref[...]

    x = x_ref[...]
    h = _rmsnorm(x, g_ref[...]).astype(BF16)
    z = _dot(h, win_ref[...])
    p = z[:, :d_pool]
    u = jax.nn.gelu(z[:, d_pool:d_pool + d_gate])
    vpre = jax.nn.gelu(z[:, d_pool + d_gate:])
    mu = jnp.mean(vpre, axis=-1, keepdims=True)
    vc = vpre - mu
    var = jnp.mean(vc * vc, axis=-1, keepdims=True)
    v = vc * lax.rsqrt(var + EPS) * lng_ref[...] + lnb_ref[...]
    if keep_v:
        v_ref[...] = v

    pos = pos0 + j * seg_rows + lax.broadcasted_iota(jnp.int32, (seg_rows, pg), 0)
    pd_segs = []
    for s in range(n_seg):
        ps = p[s * seg_rows:(s + 1) * seg_rows]
        ext = jnp.concatenate([hist_sc[s], ps], axis=0)
        hist_sc[s] = ps[seg_rows - HIST_ROWS:]
        cols = []
        for gi, w in enumerate(POOL_WINDOWS):
            acc = ext[:, gi * pg:(gi + 1) * pg]
            span = 1
            while span < w:
                acc = acc + pltpu.roll(acc, span, 0)
                span *= 2
            cnt = jnp.minimum(pos + 1, w).astype(F32)
            cols.append(acc[HIST_ROWS:] / cnt - ps[:, gi * pg:(gi + 1) * pg])
        pd_segs.append(jnp.concatenate(cols, axis=1))
    pd = pd_segs[0] if n_seg == 1 else jnp.concatenate(pd_segs, axis=0)
    hist_out_ref[...] = hist_sc[...]

    pdb = pd.astype(BF16)
    parts = []
    for gi in range(len(POOL_WINDOWS)):
        sl = slice(gi * pg, (gi + 1) * pg)
        parts.append(_dot(pdb[:, sl], poolw_ref[gi]) * pscale_ref[:, sl])
    vb = v.astype(BF16)
    gate_rows = wbig_ref.shape[1]
    for hi in range(n_heads):
        sl = slice(hi * gh, (hi + 1) * gh)
        mixed = [_dot(wbig_ref[hi], vb[r0:r0 + gate_rows, sl]) + bias_ref[:, sl]
                 for r0 in range(0, x.shape[0], gate_rows)]
        mixed = mixed[0] if len(mixed) == 1 else jnp.concatenate(mixed, axis=0)
        parts.append(u[:, sl] * mixed)
    cat = jnp.concatenate(parts, axis=1).astype(BF16)
    x = x + _dot(cat, wout_ref[...])
    if attn_heads:
        ga_ref, wq_ref, wo_ref, k_ref, v_ref2 = attn_refs
        hd = x.shape[-1] // attn_heads
        q = _queries(x, ga_ref, wq_ref, hd)
        kb = k_ref[...].astype(BF16)
        vb2 = v_ref2[...].astype(BF16)
        o = _attend(q, lambda hi: kb[:, hi * hd:(hi + 1) * hd],
                    lambda hi: vb2[:, hi * hd:(hi + 1) * hd], attn_heads, hd)
        x = x + _dot(o.astype(BF16), wo_ref[...])
    o_ref[...] = x


def _mixer(x, in_place, n_b, n_j, n_seg, seg_rows, pos0, hist, layer, w, wbig, bias, keep_v,
           cast=(), attn=None):
    d = x.shape[1]
    rows = n_seg * seg_rows
    d_in = w["w_in"].shape[-1]
    d_pool = w["pool_scale"].shape[-1]
    d_gate = w["gm_ln_g"].shape[-1]
    vec = lambda a: a[layer].reshape(1, -1)
    xspec = pl.BlockSpec((rows, d), lambda b, j: (b * n_j + j, 0))
    hspec = pl.BlockSpec((n_seg, HIST_ROWS, d_pool), lambda b, j: (b, 0, 0))
    in_specs = [xspec, _const_spec((1, d)), _const_spec((d, d_in), layer),
                _const_spec(w["pool_w"].shape[1:], layer),
                _const_spec((1, d_pool)), _const_spec((1, d_gate)), _const_spec((1, d_gate)),
                _const_spec(wbig.shape), _const_spec(bias.shape), _const_spec((d, d), layer),
                hspec]
    out_specs = [xspec, hspec]
    out_shape = [jax.ShapeDtypeStruct(x.shape, F32), jax.ShapeDtypeStruct(hist.shape, F32)]
    if keep_v:
        out_specs.append(pl.BlockSpec((rows, d_gate), lambda b, j: (b * n_j + j, 0)))
        out_shape.append(jax.ShapeDtypeStruct((x.shape[0], d_gate), F32))
    c_in, c_out, c_shapes = _side_cast(cast, n_b * n_j, lambda b, j: b * n_j + j)
    attn_args, attn_heads = [], 0
    if attn is not None:
        mem_k, mem_v, attn_heads = attn
        kvspec = pl.BlockSpec((None, None, mem_k.shape[2], d), lambda b, j: (layer, b, 0, 0))
        in_specs += [_const_spec((1, d)), _const_spec((d, d), layer), _const_spec((d, d), layer),
                     kvspec, kvspec]
        attn_args = [vec(w["norm_xa_g"]), w["xa_wq"], w["xa_wo"], mem_k, mem_v]
    return pl.pallas_call(
        functools.partial(_mixer_kernel, n_seg=n_seg, seg_rows=seg_rows, pos0=pos0, keep_v=keep_v,
                          n_cast=len(cast), attn_heads=attn_heads),
        grid=(n_b, n_j),
        in_specs=in_specs + c_in,
        out_specs=out_specs + c_out,
        out_shape=out_shape + c_shapes,
        scratch_shapes=[pltpu.VMEM((n_seg, HIST_ROWS, d_pool), F32)],
        input_output_aliases={0: 0} if in_place else {},
        compiler_params=_params(("arbitrary", "arbitrary")),
        name="mixer_attention" if attn_heads else "token_mixer",
    )(x, vec(w["norm_mix_g"]), w["w_in"], w["pool_w"], vec(w["pool_scale"]),
      vec(w["gm_ln_g"]), vec(w["gm_ln_b"]), wbig, bias, w["w_mix_out"], hist, *attn_args, *cast)


def _gate_operands(ws, bs, rows, chunk):
    n_heads = ws.shape[0]
    gh = GMLP_CHUNK
    tri = jnp.tril(jnp.ones((chunk, chunk), dtype=bool))
    wc = jnp.where(tri[None], ws[:, :chunk, :chunk], 0)
    eye = jnp.eye(rows // chunk, dtype=ws.dtype)
    wbig = jnp.einsum("ab,hts->hatbs", eye, wc).reshape(n_heads, rows, rows).astype(BF16)
    bias = jnp.tile(bs[:, :chunk].T, (rows // chunk, 1))
    bias = jnp.repeat(bias, gh, axis=1)
    return wbig, bias


def _queries(x, g_ref, wq_ref, hd):
    h = _rmsnorm(x, g_ref[...]).astype(BF16)
    return (_dot(h, wq_ref[...]) * (hd ** -0.5)).astype(BF16)


def _attend(qs, k_head, v_head, n_heads, hd):
    outs = []
    for hi in range(n_heads):
        s = lax.dot_general(qs[:, hi * hd:(hi + 1) * hd], k_head(hi), (((1,), (1,)), ((), ())),
                            preferred_element_type=F32)
        e = jnp.exp(s - jnp.max(s, axis=-1, keepdims=True))
        prob = e * (1.0 / jnp.sum(e, axis=-1, keepdims=True))
        outs.append(_dot(prob.astype(BF16), v_head(hi)))
    return jnp.concatenate(outs, axis=1)


def _attn_cached_kernel(x_ref, g_ref, wq_ref, wo_ref, k_hbm, v_hbm, o_ref, kbuf, vbuf, sem, *,
                        layer, streams):
    i = pl.program_id(0)
    n_heads, hd = k_hbm.shape[-2:]
    slot = i % 2

    def copies(step, to_slot):
        src = pl.ds(step * streams, streams)
        return [pltpu.make_async_copy(hbm.at[layer, src, :, hi, :], buf.at[to_slot, hi],
                                      sem.at[to_slot])
                for hbm, buf in ((k_hbm, kbuf), (v_hbm, vbuf)) for hi in range(n_heads)]

    @pl.when(i == 0)
    def _():
        for c in copies(0, 0):
            c.start()

    @pl.when(i + 1 < pl.num_programs(0))
    def _():
        for c in copies(i + 1, 1 - slot):
            c.start()

    x = x_ref[...]
    q = _queries(x, g_ref, wq_ref, hd)
    for c in copies(i, slot):
        c.wait()
    seg = x.shape[0] // streams
    outs = []
    for hi in range(n_heads):
        qh = q[:, hi * hd:(hi + 1) * hd].reshape(streams, seg, hd)
        s = jnp.einsum("sld,smd->slm", qh, kbuf[slot, hi].astype(BF16),
                       preferred_element_type=F32)
        e = jnp.exp(s - jnp.max(s, axis=-1, keepdims=True))
        prob = e * (1.0 / jnp.sum(e, axis=-1, keepdims=True))
        oh = jnp.einsum("slm,smd->sld", prob.astype(BF16), vbuf[slot, hi].astype(BF16),
                        preferred_element_type=F32)
        outs.append(oh.reshape(streams * seg, hd))
    o = jnp.concatenate(outs, axis=1).astype(BF16)
    o_ref[...] = x + _dot(o, wo_ref[...])


def _attention_cached(x, streams, cache_k, cache_v, layer, w):
    d = x.shape[1]
    n_streams, n_mem, n_heads, hd = cache_k.shape[1:]
    rows = x.shape[0] // n_streams * streams
    xspec = pl.BlockSpec((rows, d), lambda i: (i, 0))
    hbm = pl.BlockSpec(memory_space=pl.ANY)
    buf = pltpu.VMEM((2, n_heads, streams, n_mem, hd), F32)
    return pl.pallas_call(
        functools.partial(_attn_cached_kernel, layer=layer, streams=streams),
        grid=(n_streams // streams,),
        in_specs=[xspec, _const_spec((1, d)), _const_spec((d, d), layer),
                  _const_spec((d, d), layer), hbm, hbm],
        out_specs=xspec,
        out_shape=jax.ShapeDtypeStruct(x.shape, F32),
        scratch_shapes=[buf, buf, pltpu.SemaphoreType.DMA((2,))],
        input_output_aliases={0: 0},
        compiler_params=_params(("arbitrary",)),
        name="cached_attention",
    )(x, w["norm_xa_g"][layer].reshape(1, d), w["xa_wq"], w["xa_wo"], cache_k, cache_v)


def _swiglu_into(o_ref, hb, wg_ref, wu_ref, wd_ref, base):
    d_ff = wg_ref.shape[-1]
    for c in range(d_ff // FF_CHUNK):
        sl = slice(c * FF_CHUNK, (c + 1) * FF_CHUNK)
        a = (jax.nn.silu(_dot(hb, wg_ref[:, sl])) * _dot(hb, wu_ref[:, sl])).astype(BF16)
        part = _dot(a, wd_ref[sl, :])
        if c == 0:
            o_ref[...] = part if base is None else base + part
        else:
            o_ref[...] += part


def _ffn_kernel(x_ref, g_ref, wg_ref, wu_ref, wd_ref, gf_ref, *rest, final, n_cast):
    cast_in, o_ref, cast_out = rest[:n_cast], rest[n_cast], rest[n_cast + 1:]
    _run_side_cast(cast_in, cast_out)
    x = x_ref[...]
    hb = _rmsnorm(x, g_ref[...]).astype(BF16)
    _swiglu_into(o_ref, hb, wg_ref, wu_ref, wd_ref, x)
    if final:
        o_ref[...] = _rmsnorm(o_ref[...], gf_ref[...])


def _dense_ffn(x, g, wg, wu, wd, g_final, final, cast=()):
    t, d = x.shape
    d_ff = wg.shape[-1]
    assert t % FFN_ROWS == 0 and d_ff % FF_CHUNK == 0
    n = t // FFN_ROWS
    xspec = pl.BlockSpec((FFN_ROWS, d), lambda i: (i, 0))
    c_in, c_out, c_shapes = _side_cast(cast, n, lambda i: i)
    wspec = lambda shape: _const_spec(shape, single_buffer=True)
    return pl.pallas_call(
        functools.partial(_ffn_kernel, final=final, n_cast=len(cast)),
        grid=(n,),
        in_specs=[xspec, _const_spec((1, d)), wspec((d, d_ff)), wspec((d, d_ff)),
                  wspec((d_ff, d)), _const_spec((1, d))] + c_in,
        out_specs=[xspec] + c_out,
        out_shape=[jax.ShapeDtypeStruct((t, d), F32)] + c_shapes,
        input_output_aliases={0: 0},
        compiler_params=_params(("arbitrary",)),
        name="dense_ffn",
    )(x, g.reshape(1, d), wg, wu, wd, g_final.reshape(1, d), *cast)


def _two_source_specs(rows, d, n_a):
    return [pl.BlockSpec((rows, d), lambda i, *_: (jnp.minimum(i, n_a - 1), 0)),
            pl.BlockSpec((rows, d), lambda i, *_: (jnp.maximum(i - n_a, 0), 0))]


def _router_kernel(xa_ref, xb_ref, g_ref, rw_ref, rb_ref, before_ref, sel_ref, gate_ref, cnt_ref,
                   carry, *, n_a):
    i = pl.program_id(0)

    @pl.when(i == 0)
    def _():
        carry[...] = jnp.zeros_like(carry)

    x = jnp.where(i < n_a, xa_ref[...], xb_ref[...])
    h = _rmsnorm(x, g_ref[...])
    rw = rw_ref[...]
    h_hi, w_hi = h.astype(BF16), rw.astype(BF16)
    h_lo = (h - h_hi.astype(F32)).astype(BF16)
    w_lo = (rw - w_hi.astype(F32)).astype(BF16)
    logits = _dot(h_hi, w_hi) + (_dot(h_hi, w_lo) + _dot(h_lo, w_hi)) + rb_ref[...]
    lane_i = lax.broadcasted_iota(jnp.int32, logits.shape, 1)
    lane = lane_i.astype(F32)
    m0 = jnp.max(logits, axis=-1, keepdims=True)
    e0 = jnp.min(jnp.where(logits == m0, lane, float(LANES)), axis=-1, keepdims=True)
    rest = jnp.where(lane == e0, -jnp.inf, logits)
    m1 = jnp.max(rest, axis=-1, keepdims=True)
    e1 = jnp.min(jnp.where(rest == m1, lane, float(LANES)), axis=-1, keepdims=True)
    t = jnp.exp(m1 - m0)
    g0 = 1.0 / (1.0 + t)
    g1 = t * g0
    hot0 = (lane == e0).astype(F32)
    hot1 = (lane == e1).astype(F32)
    both = hot0 + hot1
    prior = _dot(before_ref[...], both.astype(BF16)) + carry[...]
    rank0 = jnp.sum(hot0 * prior, axis=-1, keepdims=True)
    rank1 = jnp.sum(hot1 * prior, axis=-1, keepdims=True)
    carry[...] += jnp.sum(both, axis=0, keepdims=True)
    cnt_ref[...] = carry[...]
    sel = jnp.where(lane_i == 0, e0, jnp.where(lane_i == 1, e1,
                    jnp.where(lane_i == 2, rank0, jnp.where(lane_i == 3, rank1, 0.0))))
    sel_ref[...] = jnp.transpose(sel)[:SUBLANES]
    gate_ref[...] = jnp.where(lane_i == 0, g0, jnp.where(lane_i == 1, g1, 0.0))


def _router(xa, xb, g, rw_pad, rb_pad):
    d = xa.shape[1]
    rows = FFN_ROWS
    n_a, n_b = xa.shape[0] // rows, xb.shape[0] // rows
    t = xa.shape[0] + xb.shape[0]
    before = jnp.tril(jnp.ones((rows, rows), F32), -1).astype(BF16)
    return pl.pallas_call(
        functools.partial(_router_kernel, n_a=n_a),
        grid=(n_a + n_b,),
        in_specs=_two_source_specs(rows, d, n_a) + [
            _const_spec((1, d)), _const_spec((d, LANES)), _const_spec((1, LANES)),
            _const_spec((rows, rows))],
        out_specs=[pl.BlockSpec((SUBLANES, rows), lambda i: (0, i)),
                   pl.BlockSpec((rows, LANES), lambda i: (i, 0)), _const_spec((1, LANES))],
        out_shape=[jax.ShapeDtypeStruct((SUBLANES, t), F32),
                   jax.ShapeDtypeStruct((t, LANES), F32),
                   jax.ShapeDtypeStruct((1, LANES), F32)],
        scratch_shapes=[pltpu.VMEM((1, LANES), F32)],
        compiler_params=_params(("arbitrary",)),
        name="moe_router",
    )(xa, xb, g.reshape(1, d), rw_pad, rb_pad, before)


def _dispatch_kernel(meta_ref, xa_ref, xb_ref, p0_ref, p1_ref, xs_ref, zero_sc, sem, zsem, *,
                     n_a, n_exp, min_tiles):
    i = pl.program_id(0)
    rows = xa_ref.shape[0]
    tile = zero_sc.shape[0]
    n_tiles = xs_ref.shape[0] // tile

    def zero_copy(row_end):
        start = pl.multiple_of(row_end - tile, tile)
        return pltpu.make_async_copy(zero_sc, xs_ref.at[pl.ds(start, tile)], zsem)

    fills = [(meta_ref[n_exp + e] > 0, meta_ref[e]) for e in range(n_exp)]
    fills += [(k >= meta_ref[2 * n_exp], (k + 1) * tile) for k in range(min_tiles, n_tiles)]

    @pl.when(i == 0)
    def _():
        zero_sc[...] = jnp.zeros_like(zero_sc)
        for needed, row_end in fills:
            @pl.when(needed)
            def _():
                zero_copy(row_end).start()
        for needed, row_end in fills:
            @pl.when(needed)
            def _():
                zero_copy(row_end).wait()

    def scatter_rows(x_ref):
        def start(grp, c):
            base = pl.multiple_of(grp * SUBLANES, SUBLANES)
            for u in range(SUBLANES):
                src = x_ref.at[pl.ds(base + u, 1)]
                pltpu.make_async_copy(src, xs_ref.at[pl.ds(p0_ref[base + u], 1)], sem).start(0)
                pltpu.make_async_copy(src, xs_ref.at[pl.ds(p1_ref[base + u], 1)], sem).start(1)
            return c

        lax.fori_loop(0, rows // SUBLANES, start, 0)
        for _ in range(TOP_K):
            pltpu.make_async_copy(x_ref, xs_ref.at[pl.ds(0, rows)], sem).wait()

    @pl.when(i < n_a)
    def _():
        scatter_rows(xa_ref)

    @pl.when(i >= n_a)
    def _():
        scatter_rows(xb_ref)


def _dispatch(xa, xb, pos0, pos1, meta, n_exp, n_sorted, tile):
    d = xa.shape[1]
    rows = MOVE_ROWS
    n_a, n_b = xa.shape[0] // rows, xb.shape[0] // rows
    t = xa.shape[0] + xb.shape[0]
    smem_rows = pl.BlockSpec((rows,), lambda i, *_: (i,), memory_space=pltpu.SMEM)
    min_tiles = -(-TOP_K * t // tile)
    return pl.pallas_call(
        functools.partial(_dispatch_kernel, n_a=n_a, n_exp=n_exp, min_tiles=min_tiles),
        grid_spec=pltpu.PrefetchScalarGridSpec(
            num_scalar_prefetch=1,
            grid=(n_a + n_b,),
            in_specs=_two_source_specs(rows, d, n_a) + [smem_rows, smem_rows],
            out_specs=pl.BlockSpec(memory_space=pl.ANY),
            scratch_shapes=[pltpu.VMEM((tile, d), F32), pltpu.SemaphoreType.DMA,
                            pltpu.SemaphoreType.DMA]),
        out_shape=jax.ShapeDtypeStruct((n_sorted, d), F32),
        compiler_params=_params(("arbitrary",)),
        name="moe_dispatch",
    )(meta, xa, xb, pos0, pos1)


def _expert_kernel(te_ref, nu_ref, xs_ref, g_ref, wg_ref, wu_ref, wd_ref, ys_ref):
    used = pl.program_id(0) < nu_ref[0]

    @pl.when(used)
    def _():
        hb = _rmsnorm(xs_ref[...], g_ref[...]).astype(BF16)
        _swiglu_into(ys_ref, hb, wg_ref, wu_ref, wd_ref, None)

    @pl.when(jnp.logical_not(used))
    def _():
        ys_ref[...] = jnp.zeros_like(ys_ref)


def _expert_ffn(xs, g, tile_expert, n_used, wg, wu, wd, tile):
    n_sorted, d = xs.shape
    d_ff = wg.shape[-1]
    rowmap = lambda i, te, nu: (jnp.minimum(i, nu[0] - 1), 0)
    wmap = lambda i, te, nu: (te[i], 0, 0)
    return pl.pallas_call(
        _expert_kernel,
        grid_spec=pltpu.PrefetchScalarGridSpec(
            num_scalar_prefetch=2,
            grid=(n_sorted // tile,),
            in_specs=[pl.BlockSpec((tile, d), rowmap),
                      pl.BlockSpec((1, d), lambda i, te, nu: (0, 0)),
                      pl.BlockSpec((None, d, d_ff), wmap),
                      pl.BlockSpec((None, d, d_ff), wmap),
                      pl.BlockSpec((None, d_ff, d), wmap)],
            out_specs=pl.BlockSpec((tile, d), lambda i, te, nu: (i, 0))),
        out_shape=jax.ShapeDtypeStruct((n_sorted, d), F32),
        compiler_params=_params(("arbitrary",)),
        name="expert_ffn",
    )(tile_expert, n_used, xs, g.reshape(1, d), wg, wu, wd)


def _combine_kernel(x_ref, p0_ref, p1_ref, p0n_ref, p1n_ref, gate_ref, gf_ref, ys_ref, o_ref, buf,
                    sem, *, final):
    i = pl.program_id(0)
    rows = x_ref.shape[0]
    slot = i % 2

    def gather(pa_ref, pb_ref, to_slot):
        def start(grp, c):
            base = pl.multiple_of(grp * SUBLANES, SUBLANES)
            for u in range(SUBLANES):
                r = base + u
                for k, p_ref in enumerate((pa_ref, pb_ref)):
                    pltpu.make_async_copy(ys_ref.at[pl.ds(p_ref[r], 1)],
                                          buf.at[to_slot, k, pl.ds(r, 1)], sem.at[to_slot]).start(k)
            return c

        lax.fori_loop(0, rows // SUBLANES, start, 0)

    @pl.when(i == 0)
    def _():
        gather(p0_ref, p1_ref, 0)

    @pl.when(i + 1 < pl.num_programs(0))
    def _():
        gather(p0n_ref, p1n_ref, 1 - slot)

    for k in range(TOP_K):
        pltpu.make_async_copy(ys_ref.at[pl.ds(0, rows)], buf.at[slot, k], sem.at[slot]).wait()
    gates = gate_ref[...]
    out = x_ref[...] + gates[:, 0:1] * buf[slot, 0] + gates[:, 1:2] * buf[slot, 1]
    if final:
        out = _rmsnorm(out, gf_ref[...])
    o_ref[...] = out


def _combine(x, off_rows, pos0, pos1, gates, ys, g_final, final):
    n_rows, d = x.shape
    rows = MOVE_ROWS
    off = off_rows // rows
    n = n_rows // rows
    smem_rows = pl.BlockSpec((rows,), lambda i: (off + i,), memory_space=pltpu.SMEM)
    smem_next = pl.BlockSpec((rows,), lambda i: (off + jnp.minimum(i + 1, n - 1),),
                             memory_space=pltpu.SMEM)
    xspec = pl.BlockSpec((rows, d), lambda i: (i, 0))
    return pl.pallas_call(
        functools.partial(_combine_kernel, final=final),
        grid=(n,),
        in_specs=[xspec, smem_rows, smem_rows, smem_next, smem_next,
                  pl.BlockSpec((rows, LANES), lambda i: (off + i, 0)), _const_spec((1, d)),
                  pl.BlockSpec(memory_space=pl.ANY)],
        out_specs=xspec,
        out_shape=jax.ShapeDtypeStruct((n_rows, d), F32),
        scratch_shapes=[pltpu.VMEM((2, TOP_K, rows, d), F32), pltpu.SemaphoreType.DMA((2,))],
        compiler_params=_params(("arbitrary",)),
        name="moe_combine",
    )(x, pos0, pos1, pos0, pos1, gates, g_final.reshape(1, d), ys)


def _moe_layer(xa, xb, g, rw, rb, wg, wu, wd, g_final, final):
    d = xa.shape[1]
    t = xa.shape[0] + xb.shape[0]
    n_exp = rw.shape[-1]
    tile = FFN_ROWS
    rw_pad = jnp.zeros((d, LANES), F32).at[:, :n_exp].set(rw)
    rb_pad = jnp.full((1, LANES), -jnp.inf, F32).at[0, :n_exp].set(rb)
    sel, gates, counts = _router(xa, xb, g, rw_pad, rb_pad)

    counts = counts[0, :n_exp].astype(jnp.int32)
    gpad = (counts + tile - 1) // tile * tile
    gend = jnp.cumsum(gpad)
    gstart = gend - gpad
    sel = sel.astype(jnp.int32)
    pos0 = gstart[sel[0]] + sel[2]
    pos1 = gstart[sel[1]] + sel[3]
    n_tiles = (TOP_K * t + n_exp * (tile - 1)) // tile
    tile_ids = jnp.arange(n_tiles, dtype=jnp.int32)
    tile_expert = jnp.minimum(
        jnp.sum((gend // tile)[None, :] <= tile_ids[:, None], axis=1), n_exp - 1).astype(jnp.int32)
    n_used = (gend[-1:] // tile).astype(jnp.int32)

    meta = jnp.concatenate([gend, gpad, n_used]).astype(jnp.int32)
    xs = _dispatch(xa, xb, pos0, pos1, meta, n_exp, n_tiles * tile, tile)
    ys = _expert_ffn(xs, g, tile_expert, n_used, wg, wu, wd, tile)
    return (_combine(xa, 0, pos0, pos1, gates, ys, g_final, final),
            _combine(xb, xa.shape[0], pos0, pos1, gates, ys, g_final, final))


def kernel(x_prompt, x_sample, cache_mem_k, cache_mem_v, state_pool, mem_prompt, norm_mix_g, w_in, pool_w, pool_scale, gm_ln_g, gm_ln_b, gm_ws, gm_bs, w_mix_out, norm_xa_g, norm_mem_g, xa_wq, xa_wk, xa_wv, xa_wo, norm_ffn_g, ffn_wg, ffn_wu, ffn_wd, moe_router_w, moe_router_b, moe_wg, moe_wu, moe_wd, norm_final_g):
    batch, seq, d = x_prompt.shape
    dec_batch, dec_seq, _ = x_sample.shape
    depth = norm_mix_g.shape[0]
    n_mem, n_mem_heads, mem_hd = cache_mem_k.shape[2:]
    d_pool = pool_scale.shape[-1]
    d_gate = gm_ln_g.shape[-1]
    rows_p, rows_s = batch * seq, dec_batch * dec_seq
    seg_per_tile = GATE_ROWS // dec_seq
    assert seq % MIX_ROWS == 0 and MIX_ROWS % GATE_ROWS == 0
    assert GATE_ROWS % GMLP_CHUNK == 0
    assert GATE_ROWS % dec_seq == 0 and dec_batch % seg_per_tile == 0 and dec_seq >= HIST_ROWS
    assert dec_seq <= GMLP_CHUNK and PAST_LEN % GMLP_CHUNK == 0
    assert dec_batch % ATTN_SAMPLE_STREAMS == 0
    assert rows_p % FFN_ROWS == 0 and rows_s % FFN_ROWS == 0
    assert rows_p % MOVE_ROWS == 0 and rows_s % MOVE_ROWS == 0

    w = dict(norm_mix_g=norm_mix_g, w_in=w_in.astype(BF16), pool_w=pool_w.astype(BF16),
             pool_scale=pool_scale, gm_ln_g=gm_ln_g, gm_ln_b=gm_ln_b,
             w_mix_out=w_mix_out.astype(BF16), norm_xa_g=norm_xa_g,
             xa_wq=xa_wq.astype(BF16), xa_wo=xa_wo.astype(BF16))

    mk, mv, new_mem_k, new_mem_v = _memory_kv(mem_prompt, norm_mem_g, xa_wk, xa_wv, n_mem_heads)
    pk, pv = mk.reshape(depth, batch, n_mem, d), mv.reshape(depth, batch, n_mem, d)

    hist_p0 = jnp.zeros((batch, HIST_ROWS, d_pool), F32)
    hist_s = jnp.pad(state_pool, ((0, 0), (0, 0), (HIST_ROWS - state_pool.shape[2], 0), (0, 0)))
    keep = HIST_ROWS - state_pool.shape[2]

    xp = x_prompt.reshape(rows_p, d)
    xs = x_sample.reshape(rows_s, d)
    pool_p, pool_s, v_rows = [], [], []
    n_exp, _, d_ff = moe_wg.shape[1:]
    expert_w = {}
    for l in range(depth):
        wbig_p, bias_p = _gate_operands(gm_ws[l], gm_bs[l], GATE_ROWS, GMLP_CHUNK)
        wbig_s, bias_s = _gate_operands(gm_ws[l], gm_bs[l], GATE_ROWS, dec_seq)
        if l % 2 == 1:
            mix_cast = [moe_wd[l // 2].reshape(n_exp * d_ff, d)]
        else:
            mix_cast = [ffn_wg[l // 2], ffn_wu[l // 2], ffn_wd[l // 2]]
        xp, hp, *mix_done = _mixer(xp, l > 0, batch, seq // MIX_ROWS, 1, MIX_ROWS, 0, hist_p0, l,
                                   w, wbig_p, bias_p, False, mix_cast, (pk, pv, n_mem_heads))
        if l % 2 == 1:
            expert_w["wd"] = mix_done[0].reshape(n_exp, d_ff, d)
        else:
            ffn_b = mix_done
        xs, hs, vr = _mixer(xs, l > 0, dec_batch // seg_per_tile, 1, seg_per_tile, dec_seq,
                            PAST_LEN, hist_s[l], l, w, wbig_s, bias_s, True)
        pool_p.append(hp[:, keep:])
        pool_s.append(hs[:, keep:])
        v_rows.append(vr.reshape(dec_batch, dec_seq, d_gate))

        xs = _attention_cached(xs, ATTN_SAMPLE_STREAMS, cache_mem_k, cache_mem_v, l, w)

        final = l == depth - 1
        if l % 2 == 0:
            ffn_cast = ([moe_wg[l // 2].reshape(n_exp * d, d_ff),
                         moe_wu[l // 2].reshape(n_exp * d, d_ff)] if l + 1 < depth else [])
            xp, *ffn_done = _dense_ffn(xp, norm_ffn_g[l], *ffn_b, norm_final_g, final, ffn_cast)
            xs, = _dense_ffn(xs, norm_ffn_g[l], *ffn_b, norm_final_g, final)
            if ffn_done:
                expert_w["wg"], expert_w["wu"] = (a.reshape(n_exp, d, d_ff) for a in ffn_done)
        else:
            i = l // 2
            xp, xs = _moe_layer(xp, xs, norm_ffn_g[l], moe_router_w[i], moe_router_b[i],
                                expert_w["wg"], expert_w["wu"], expert_w["wd"], norm_final_g, final)

    return (xp.reshape(batch, seq, d), xs.reshape(dec_batch, dec_seq, d),
            new_mem_k, new_mem_v, jnp.stack(pool_p), jnp.stack(pool_s), jnp.stack(v_rows))
```

```python
import functools

import jax
import jax.numpy as jnp
from jax import lax
from jax.experimental import pallas as pl
from jax.experimental.pallas import tpu as pltpu

EPS = 1e-6
PAST_LEN = 4096
POOL_WINDOWS = (2, 4, 8, 16)
HIST_ROWS = 16
GMLP_CHUNK = 128
LANES = 128
SUBLANES = 8
TOP_K = 2

V7X_VMEM_BYTES = 64 * 1024 * 1024
VMEM_LIMIT = V7X_VMEM_BYTES - 8 * 1024 * 1024

MIX_ROWS = 1024
GATE_ROWS = 256
ATTN_SAMPLE_STREAMS = 8
FFN_ROWS = 512
FF_CHUNK = 256
MOVE_ROWS = 512

BF16 = jnp.bfloat16
F32 = jnp.float32


def _rmsnorm(x, g):
    return x * lax.rsqrt(jnp.mean(x * x, axis=-1, keepdims=True) + EPS) * g


def _dot(a, b):
    return jnp.dot(a, b, preferred_element_type=F32)


def _params(sem, vmem=VMEM_LIMIT):
    return pltpu.CompilerParams(dimension_semantics=sem, vmem_limit_bytes=vmem)


def _const_spec(shape, layer=None, single_buffer=False):
    nd = len(shape)
    mode = dict(pipeline_mode=pl.Buffered(1)) if single_buffer else {}
    if layer is None:
        return pl.BlockSpec(shape, lambda *_: (0,) * nd, **mode)
    return pl.BlockSpec((None,) + shape, lambda *_: (layer,) + (0,) * nd, **mode)


def _side_cast(arrays, n_steps, step_of):
    specs, shapes = [], []
    for a in arrays:
        rows, cols = a.shape
        n_blk = n_steps
        while rows % (n_blk * 2 * SUBLANES):
            n_blk -= 1
        imap = lambda *ids, n_blk=n_blk: (jnp.minimum(step_of(*ids), n_blk - 1), 0)
        specs.append(pl.BlockSpec((rows // n_blk, cols), imap))
        shapes.append(jax.ShapeDtypeStruct(a.shape, BF16))
    return specs, list(specs), shapes


def _run_side_cast(in_refs, out_refs):
    for src, dst in zip(in_refs, out_refs):
        dst[...] = src[...].astype(BF16)


def _memkv_kernel(mem_ref, g_ref, wk_ref, wv_ref, k_ref, v_ref, k5_ref, v5_ref, sem):
    l = pl.program_id(0)
    m = _rmsnorm(mem_ref[...], g_ref[...]).astype(BF16)
    k_ref[0] = _dot(m, wk_ref[...].astype(BF16))
    v_ref[0] = _dot(m, wv_ref[...].astype(BF16))
    depth, n_b, n_mem, n_heads, hd = k5_ref.shape
    for layer in range(depth):
        @pl.when(l == layer)
        def _():
            copies = [pltpu.make_async_copy(src.at[0, pl.ds(b * n_mem, n_mem), pl.ds(h * hd, hd)],
                                            dst.at[layer, b, :, h, :], sem)
                      for src, dst in ((k_ref, k5_ref), (v_ref, v5_ref))
                      for b in range(n_b) for h in range(n_heads)]
            for c in copies:
                c.start()
            for c in copies:
                c.wait()


def _memory_kv(mem, norm_g, wk, wv, n_heads):
    depth, d = norm_g.shape
    n_b, n_mem, _ = mem.shape
    rows = n_b * n_mem
    lay = lambda l: (l, 0, 0)
    five_d = jax.ShapeDtypeStruct((depth, n_b, n_mem, n_heads, d // n_heads), F32)
    return pl.pallas_call(
        _memkv_kernel,
        grid=(depth,),
        in_specs=[pl.BlockSpec((rows, d), lambda l: (0, 0)),
                  pl.BlockSpec((None, 1, d), lay),
                  pl.BlockSpec((None, d, d), lay),
                  pl.BlockSpec((None, d, d), lay)],
        out_specs=[pl.BlockSpec((1, rows, d), lay)] * 2
                  + [pl.BlockSpec(memory_space=pl.ANY)] * 2,
        out_shape=[jax.ShapeDtypeStruct((depth, rows, d), F32)] * 2 + [five_d] * 2,
        scratch_shapes=[pltpu.SemaphoreType.DMA],
        compiler_params=_params(("arbitrary",)),
        name="memory_kv",
    )(mem.reshape(rows, d), norm_g.reshape(depth, 1, d), wk, wv)


def _mixer_kernel(x_ref, g_ref, win_ref, poolw_ref, pscale_ref, lng_ref, lnb_ref, wbig_ref,
                  bias_ref, wout_ref, hist_ref, *rest, n_seg, seg_rows, pos0, keep_v, n_cast,
                  attn_heads):
    attn_refs, rest = (rest[:5], rest[5:]) if attn_heads else ((), rest)
    cast_in, rest = rest[:n_cast], rest[n_cast:]
    o_ref, hist_out_ref = rest[:2]
    v_ref = rest[2] if keep_v else None
    cast_out = rest[len(rest) - 1 - n_cast:-1]
    hist_sc = rest[-1]
    _run_side_cast(cast_in, cast_out)
    j = pl.program_id(1)
    d_pool = pscale_ref.shape[-1]
    d_gate = lng_ref.shape[-1]
    pg = d_pool // len(POOL_WINDOWS)
    n_heads = wbig_ref.shape[0]
    gh = d_gate // n_heads

    @pl.when(j == 0)
    def _():
        hist_sc[...] = hist_ref[...]

    x = x_ref[...]
    h = _rmsnorm(x, g_ref[...]).astype(BF16)
    z = _dot(h, win_ref[...])
    p = z[:, :d_pool]
    u = jax.nn.gelu(z[:, d_pool:d_pool + d_gate])
    vpre = jax.nn.gelu(z[:, d_pool + d_gate:])
    mu = jnp.mean(vpre, axis=-1, keepdims=True)
    vc = vpre - mu
    var = jnp.mean(vc * vc, axis=-1, keepdims=True)
    v = vc * lax.rsqrt(var + EPS) * lng_ref[...] + lnb_ref[...]
    if keep_v:
        v_ref[...] = v

    pos = pos0 + j * seg_rows + lax.broadcasted_iota(jnp.int32, (seg_rows, pg), 0)
    pd_segs = []
    for s in range(n_seg):
        ps = p[s * seg_rows:(s + 1) * seg_rows]
        ext = jnp.concatenate([hist_sc[s], ps], axis=0)
        hist_sc[s] = ps[seg_rows - HIST_ROWS:]
        cols = []
        for gi, w in enumerate(POOL_WINDOWS):
            acc = ext[:, gi * pg:(gi + 1) * pg]
            span = 1
            while span < w:
                acc = acc + pltpu.roll(acc, span, 0)
                span *= 2
            cnt = jnp.minimum(pos + 1, w).astype(F32)
            cols.append(acc[HIST_ROWS:] / cnt - ps[:, gi * pg:(gi + 1) * pg])
        pd_segs.append(jnp.concatenate(cols, axis=1))
    pd = pd_segs[0] if n_seg == 1 else jnp.concatenate(pd_segs, axis=0)
    hist_out_ref[...] = hist_sc[...]

    pdb = pd.astype(BF16)
    parts = []
    for gi in range(len(POOL_WINDOWS)):
        sl = slice(gi * pg, (gi + 1) * pg)
        parts.append(_dot(pdb[:, sl], poolw_ref[gi]) * pscale_ref[:, sl])
    vb = v.astype(BF16)
    gate_rows = wbig_ref.shape[1]
    for hi in range(n_heads):
        sl = slice(hi * gh, (hi + 1) * gh)
        mixed = [_dot(wbig_ref[hi], vb[r0:r0 + gate_rows, sl]) + bias_ref[:, sl]
                 for r0 in range(0, x.shape[0], gate_rows)]
        mixed = mixed[0] if len(mixed) == 1 else jnp.concatenate(mixed, axis=0)
        parts.append(u[:, sl] * mixed)
    cat = jnp.concatenate(parts, axis=1).astype(BF16)
    x = x + _dot(cat, wout_ref[...])
    if attn_heads:
        ga_ref, wq_ref, wo_ref, k_ref, v_ref2 = attn_refs
        hd = x.shape[-1] // attn_heads
        q = _queries(x, ga_ref, wq_ref, hd)
        kb = k_ref[...].astype(BF16)
        vb2 = v_ref2[...].astype(BF16)
        o = _attend(q, lambda hi: kb[:, hi * hd:(hi + 1) * hd],
                    lambda hi: vb2[:, hi * hd:(hi + 1) * hd], attn_heads, hd)
        x = x + _dot(o.astype(BF16), wo_ref[...])
    o_ref[...] = x


def _mixer(x, in_place, n_b, n_j, n_seg, seg_rows, pos0, hist, layer, w, wbig, bias, keep_v,
           cast=(), attn=None):
    d = x.shape[1]
    rows = n_seg * seg_rows
    d_in = w["w_in"].shape[-1]
    d_pool = w["pool_scale"].shape[-1]
    d_gate = w["gm_ln_g"].shape[-1]
    vec = lambda a: a[layer].reshape(1, -1)
    xspec = pl.BlockSpec((rows, d), lambda b, j: (b * n_j + j, 0))
    hspec = pl.BlockSpec((n_seg, HIST_ROWS, d_pool), lambda b, j: (b, 0, 0))
    in_specs = [xspec, _const_spec((1, d)), _const_spec((d, d_in), layer),
                _const_spec(w["pool_w"].shape[1:], layer),
                _const_spec((1, d_pool)), _const_spec((1, d_gate)), _const_spec((1, d_gate)),
                _const_spec(wbig.shape), _const_spec(bias.shape), _const_spec((d, d), layer),
                hspec]
    out_specs = [xspec, hspec]
    out_shape = [jax.ShapeDtypeStruct(x.shape, F32), jax.ShapeDtypeStruct(hist.shape, F32)]
    if keep_v:
        out_specs.append(pl.BlockSpec((rows, d_gate), lambda b, j: (b * n_j + j, 0)))
        out_shape.append(jax.ShapeDtypeStruct((x.shape[0], d_gate), F32))
    c_in, c_out, c_shapes = _side_cast(cast, n_b * n_j, lambda b, j: b * n_j + j)
    attn_args, attn_heads = [], 0
    if attn is not None:
        mem_k, mem_v, attn_heads = attn
        kvspec = pl.BlockSpec((None, None, mem_k.shape[2], d), lambda b, j: (layer, b, 0, 0))
        in_specs += [_const_spec((1, d)), _const_spec((d, d), layer), _const_spec((d, d), layer),
                     kvspec, kvspec]
        attn_args = [vec(w["norm_xa_g"]), w["xa_wq"], w["xa_wo"], mem_k, mem_v]
    return pl.pallas_call(
        functools.partial(_mixer_kernel, n_seg=n_seg, seg_rows=seg_rows, pos0=pos0, keep_v=keep_v,
                          n_cast=len(cast), attn_heads=attn_heads),
        grid=(n_b, n_j),
        in_specs=in_specs + c_in,
        out_specs=out_specs + c_out,
        out_shape=out_shape + c_shapes,
        scratch_shapes=[pltpu.VMEM((n_seg, HIST_ROWS, d_pool), F32)],
        input_output_aliases={0: 0} if in_place else {},
        compiler_params=_params(("arbitrary", "arbitrary")),
        name="mixer_attention" if attn_heads else "token_mixer",
    )(x, vec(w["norm_mix_g"]), w["w_in"], w["pool_w"], vec(w["pool_scale"]),
      vec(w["gm_ln_g"]), vec(w["gm_ln_b"]), wbig, bias, w["w_mix_out"], hist, *attn_args, *cast)


def _gate_operands(ws, bs, rows, chunk):
    n_heads = ws.shape[0]
    gh = GMLP_CHUNK
    tri = jnp.tril(jnp.ones((chunk, chunk), dtype=bool))
    wc = jnp.where(tri[None], ws[:, :chunk, :chunk], 0)
    eye = jnp.eye(rows // chunk, dtype=ws.dtype)
    wbig = jnp.einsum("ab,hts->hatbs", eye, wc).reshape(n_heads, rows, rows).astype(BF16)
    bias = jnp.tile(bs[:, :chunk].T, (rows // chunk, 1))
    bias = jnp.repeat(bias, gh, axis=1)
    return wbig, bias


def _queries(x, g_ref, wq_ref, hd):
    h = _rmsnorm(x, g_ref[...]).astype(BF16)
    return (_dot(h, wq_ref[...]) * (hd ** -0.5)).astype(BF16)


def _attend(qs, k_head, v_head, n_heads, hd):
    outs = []
    for hi in range(n_heads):
        s = lax.dot_general(qs[:, hi * hd:(hi + 1) * hd], k_head(hi), (((1,), (1,)), ((), ())),
                            preferred_element_type=F32)
        e = jnp.exp(s - jnp.max(s, axis=-1, keepdims=True))
        prob = e * (1.0 / jnp.sum(e, axis=-1, keepdims=True))
        outs.append(_dot(prob.astype(BF16), v_head(hi)))
    return jnp.concatenate(outs, axis=1)


def _attn_cached_kernel(x_ref, g_ref, wq_ref, wo_ref, k_hbm, v_hbm, o_ref, kbuf, vbuf, sem, *,
                        layer, streams):
    i = pl.program_id(0)
    n_heads, hd = k_hbm.shape[-2:]
    slot = i % 2

    def copies(step, to_slot):
        src = pl.ds(step * streams, streams)
        return [pltpu.make_async_copy(hbm.at[layer, src, :, hi, :], buf.at[to_slot, hi],
                                      sem.at[to_slot])
                for hbm, buf in ((k_hbm, kbuf), (v_hbm, vbuf)) for hi in range(n_heads)]

    @pl.when(i == 0)
    def _():
        for c in copies(0, 0):
            c.start()

    @pl.when(i + 1 < pl.num_programs(0))
    def _():
        for c in copies(i + 1, 1 - slot):
            c.start()

    x = x_ref[...]
    q = _queries(x, g_ref, wq_ref, hd)
    for c in copies(i, slot):
        c.wait()
    seg = x.shape[0] // streams
    outs = []
    for hi in range(n_heads):
        qh = q[:, hi * hd:(hi + 1) * hd].reshape(streams, seg, hd)
        s = jnp.einsum("sld,smd->slm", qh, kbuf[slot, hi].astype(BF16),
                       preferred_element_type=F32)
        e = jnp.exp(s - jnp.max(s, axis=-1, keepdims=True))
        prob = e * (1.0 / jnp.sum(e, axis=-1, keepdims=True))
        oh = jnp.einsum("slm,smd->sld", prob.astype(BF16), vbuf[slot, hi].astype(BF16),
                        preferred_element_type=F32)
        outs.append(oh.reshape(streams * seg, hd))
    o = jnp.concatenate(outs, axis=1).astype(BF16)
    o_ref[...] = x + _dot(o, wo_ref[...])


def _attention_cached(x, streams, cache_k, cache_v, layer, w):
    d = x.shape[1]
    n_streams, n_mem, n_heads, hd = cache_k.shape[1:]
    rows = x.shape[0] // n_streams * streams
    xspec = pl.BlockSpec((rows, d), lambda i: (i, 0))
    hbm = pl.BlockSpec(memory_space=pl.ANY)
    buf = pltpu.VMEM((2, n_heads, streams, n_mem, hd), F32)
    return pl.pallas_call(
        functools.partial(_attn_cached_kernel, layer=layer, streams=streams),
        grid=(n_streams // streams,),
        in_specs=[xspec, _const_spec((1, d)), _const_spec((d, d), layer),
                  _const_spec((d, d), layer), hbm, hbm],
        out_specs=xspec,
        out_shape=jax.ShapeDtypeStruct(x.shape, F32),
        scratch_shapes=[buf, buf, pltpu.SemaphoreType.DMA((2,))],
        input_output_aliases={0: 0},
        compiler_params=_params(("arbitrary",)),
        name="cached_attention",
    )(x, w["norm_xa_g"][layer].reshape(1, d), w["xa_wq"], w["xa_wo"], cache_k, cache_v)


def _swiglu_into(o_ref, hb, wg_ref, wu_ref, wd_ref, base):
    d_ff = wg_ref.shape[-1]
    for c in range(d_ff // FF_CHUNK):
        sl = slice(c * FF_CHUNK, (c + 1) * FF_CHUNK)
        a = (jax.nn.silu(_dot(hb, wg_ref[:, sl])) * _dot(hb, wu_ref[:, sl])).astype(BF16)
        part = _dot(a, wd_ref[sl, :])
        if c == 0:
            o_ref[...] = part if base is None else base + part
        else:
            o_ref[...] += part


def _ffn_kernel(x_ref, g_ref, wg_ref, wu_ref, wd_ref, gf_ref, *rest, final, n_cast):
    cast_in, o_ref, cast_out = rest[:n_cast], rest[n_cast], rest[n_cast + 1:]
    _run_side_cast(cast_in, cast_out)
    x = x_ref[...]
    hb = _rmsnorm(x, g_ref[...]).astype(BF16)
    _swiglu_into(o_ref, hb, wg_ref, wu_ref, wd_ref, x)
    if final:
        o_ref[...] = _rmsnorm(o_ref[...], gf_ref[...])


def _dense_ffn(x, g, wg, wu, wd, g_final, final, cast=()):
    t, d = x.shape
    d_ff = wg.shape[-1]
    assert t % FFN_ROWS == 0 and d_ff % FF_CHUNK == 0
    n = t // FFN_ROWS
    xspec = pl.BlockSpec((FFN_ROWS, d), lambda i: (i, 0))
    c_in, c_out, c_shapes = _side_cast(cast, n, lambda i: i)
    wspec = lambda shape: _const_spec(shape, single_buffer=True)
    return pl.pallas_call(
        functools.partial(_ffn_kernel, final=final, n_cast=len(cast)),
        grid=(n,),
        in_specs=[xspec, _const_spec((1, d)), wspec((d, d_ff)), wspec((d, d_ff)),
                  wspec((d_ff, d)), _const_spec((1, d))] + c_in,
        out_specs=[xspec] + c_out,
        out_shape=[jax.ShapeDtypeStruct((t, d), F32)] + c_shapes,
        input_output_aliases={0: 0},
        compiler_params=_params(("arbitrary",)),
        name="dense_ffn",
    )(x, g.reshape(1, d), wg, wu, wd, g_final.reshape(1, d), *cast)


def _two_source_specs(rows, d, n_a):
    return [pl.BlockSpec((rows, d), lambda i, *_: (jnp.minimum(i, n_a - 1), 0)),
            pl.BlockSpec((rows, d), lambda i, *_: (jnp.maximum(i - n_a, 0), 0))]


def _router_kernel(xa_ref, xb_ref, g_ref, rw_ref, rb_ref, before_ref, sel_ref, gate_ref, cnt_ref,
                   carry, *, n_a):
    i = pl.program_id(0)

    @pl.when(i == 0)
    def _():
        carry[...] = jnp.zeros_like(carry)

    x = jnp.where(i < n_a, xa_ref[...], xb_ref[...])
    h = _rmsnorm(x, g_ref[...])
    rw = rw_ref[...]
    h_hi, w_hi = h.astype(BF16), rw.astype(BF16)
    h_lo = (h - h_hi.astype(F32)).astype(BF16)
    w_lo = (rw - w_hi.astype(F32)).astype(BF16)
    logits = _dot(h_hi, w_hi) + (_dot(h_hi, w_lo) + _dot(h_lo, w_hi)) + rb_ref[...]
    lane_i = lax.broadcasted_iota(jnp.int32, logits.shape, 1)
    lane = lane_i.astype(F32)
    m0 = jnp.max(logits, axis=-1, keepdims=True)
    e0 = jnp.min(jnp.where(logits == m0, lane, float(LANES)), axis=-1, keepdims=True)
    rest = jnp.where(lane == e0, -jnp.inf, logits)
    m1 = jnp.max(rest, axis=-1, keepdims=True)
    e1 = jnp.min(jnp.where(rest == m1, lane, float(LANES)), axis=-1, keepdims=True)
    t = jnp.exp(m1 - m0)
    g0 = 1.0 / (1.0 + t)
    g1 = t * g0
    hot0 = (lane == e0).astype(F32)
    hot1 = (lane == e1).astype(F32)
    both = hot0 + hot1
    prior = _dot(before_ref[...], both.astype(BF16)) + carry[...]
    rank0 = jnp.sum(hot0 * prior, axis=-1, keepdims=True)
    rank1 = jnp.sum(hot1 * prior, axis=-1, keepdims=True)
    carry[...] += jnp.sum(both, axis=0, keepdims=True)
    cnt_ref[...] = carry[...]
    sel = jnp.where(lane_i == 0, e0, jnp.where(lane_i == 1, e1,
                    jnp.where(lane_i == 2, rank0, jnp.where(lane_i == 3, rank1, 0.0))))
    sel_ref[...] = jnp.transpose(sel)[:SUBLANES]
    gate_ref[...] = jnp.where(lane_i == 0, g0, jnp.where(lane_i == 1, g1, 0.0))


def _router(xa, xb, g, rw_pad, rb_pad):
    d = xa.shape[1]
    rows = FFN_ROWS
    n_a, n_b = xa.shape[0] // rows, xb.shape[0] // rows
    t = xa.shape[0] + xb.shape[0]
    before = jnp.tril(jnp.ones((rows, rows), F32), -1).astype(BF16)
    return pl.pallas_call(
        functools.partial(_router_kernel, n_a=n_a),
        grid=(n_a + n_b,),
        in_specs=_two_source_specs(rows, d, n_a) + [
            _const_spec((1, d)), _const_spec((d, LANES)), _const_spec((1, LANES)),
            _const_spec((rows, rows))],
        out_specs=[pl.BlockSpec((SUBLANES, rows), lambda i: (0, i)),
                   pl.BlockSpec((rows, LANES), lambda i: (i, 0)), _const_spec((1, LANES))],
        out_shape=[jax.ShapeDtypeStruct((SUBLANES, t), F32),
                   jax.ShapeDtypeStruct((t, LANES), F32),
                   jax.ShapeDtypeStruct((1, LANES), F32)],
        scratch_shapes=[pltpu.VMEM((1, LANES), F32)],
        compiler_params=_params(("arbitrary",)),
        name="moe_router",
    )(xa, xb, g.reshape(1, d), rw_pad, rb_pad, before)


def _dispatch_kernel(meta_ref, xa_ref, xb_ref, p0_ref, p1_ref, xs_ref, zero_sc, sem, zsem, *,
                     n_a, n_exp, min_tiles):
    i = pl.program_id(0)
    rows = xa_ref.shape[0] * SUBLANES
    tile = zero_sc.shape[0]
    n_tiles = xs_ref.shape[0] // tile

    def zero_copy(row_end):
        start = pl.multiple_of(row_end - tile, tile)
        return pltpu.make_async_copy(zero_sc, xs_ref.at[pl.ds(start, tile)], zsem)

    fills = [(meta_ref[n_exp + e] > 0, meta_ref[e]) for e in range(n_exp)]
    fills += [(k >= meta_ref[2 * n_exp], (k + 1) * tile) for k in range(min_tiles, n_tiles)]

    @pl.when(i == 0)
    def _():
        zero_sc[...] = jnp.zeros_like(zero_sc)
        for needed, row_end in fills:
            @pl.when(needed)
            def _():
                zero_copy(row_end).start()
        for needed, row_end in fills:
            @pl.when(needed)
            def _():
                zero_copy(row_end).wait()

    def scatter_rows(x_ref):
        def start(grp, c):
            base = pl.multiple_of(grp * SUBLANES, SUBLANES)
            for u in range(SUBLANES):
                src = x_ref.at[grp, pl.ds(u, 1)]
                pltpu.make_async_copy(src, xs_ref.at[pl.ds(p0_ref[base + u], 1)], sem).start(0)
                pltpu.make_async_copy(src, xs_ref.at[pl.ds(p1_ref[base + u], 1)], sem).start(1)
            return c

        lax.fori_loop(0, rows // SUBLANES, start, 0)
        for _ in range(TOP_K):
            pltpu.make_async_copy(zero_sc.at[pl.ds(0, rows)], xs_ref.at[pl.ds(0, rows)], sem).wait()

    @pl.when(i < n_a)
    def _():
        scatter_rows(xa_ref)

    @pl.when(i >= n_a)
    def _():
        scatter_rows(xb_ref)


def _dispatch(xa, xb, pos0, pos1, meta, n_exp, n_sorted, tile):
    d = xa.shape[1]
    rows = MOVE_ROWS
    n_a, n_b = xa.shape[0] // rows, xb.shape[0] // rows
    t = xa.shape[0] + xb.shape[0]
    smem_rows = pl.BlockSpec((rows,), lambda i, *_: (i,), memory_space=pltpu.SMEM)
    min_tiles = -(-TOP_K * t // tile)
    assert rows <= tile and rows % SUBLANES == 0
    grp = rows // SUBLANES
    grouped = lambda x: x.reshape(x.shape[0] // SUBLANES, SUBLANES, d)
    x_specs = [pl.BlockSpec((grp, SUBLANES, d), lambda i, *_: (jnp.minimum(i, n_a - 1), 0, 0)),
               pl.BlockSpec((grp, SUBLANES, d), lambda i, *_: (jnp.maximum(i - n_a, 0), 0, 0))]
    return pl.pallas_call(
        functools.partial(_dispatch_kernel, n_a=n_a, n_exp=n_exp, min_tiles=min_tiles),
        grid_spec=pltpu.PrefetchScalarGridSpec(
            num_scalar_prefetch=1,
            grid=(n_a + n_b,),
            in_specs=x_specs + [smem_rows, smem_rows],
            out_specs=pl.BlockSpec(memory_space=pl.ANY),
            scratch_shapes=[pltpu.VMEM((tile, d), F32), pltpu.SemaphoreType.DMA,
                            pltpu.SemaphoreType.DMA]),
        out_shape=jax.ShapeDtypeStruct((n_sorted, d), F32),
        compiler_params=_params(("arbitrary",)),
        name="moe_dispatch",
    )(meta, grouped(xa), grouped(xb), pos0, pos1)


def _expert_kernel(te_ref, nu_ref, xs_ref, g_ref, wg_ref, wu_ref, wd_ref, ys_ref):
    used = pl.program_id(0) < nu_ref[0]

    @pl.when(used)
    def _():
        hb = _rmsnorm(xs_ref[...], g_ref[...]).astype(BF16)
        _swiglu_into(ys_ref, hb, wg_ref, wu_ref, wd_ref, None)

    @pl.when(jnp.logical_not(used))
    def _():
        ys_ref[...] = jnp.zeros_like(ys_ref)


def _expert_ffn(xs, g, tile_expert, n_used, wg, wu, wd, tile):
    n_sorted, d = xs.shape
    d_ff = wg.shape[-1]
    rowmap = lambda i, te, nu: (jnp.minimum(i, nu[0] - 1), 0)
    wmap = lambda i, te, nu: (te[i], 0, 0)
    return pl.pallas_call(
        _expert_kernel,
        grid_spec=pltpu.PrefetchScalarGridSpec(
            num_scalar_prefetch=2,
            grid=(n_sorted // tile,),
            in_specs=[pl.BlockSpec((tile, d), rowmap),
                      pl.BlockSpec((1, d), lambda i, te, nu: (0, 0)),
                      pl.BlockSpec((None, d, d_ff), wmap),
                      pl.BlockSpec((None, d, d_ff), wmap),
                      pl.BlockSpec((None, d_ff, d), wmap)],
            out_specs=pl.BlockSpec((tile, d), lambda i, te, nu: (i, 0))),
        out_shape=jax.ShapeDtypeStruct((n_sorted, d), F32),
        compiler_params=_params(("arbitrary",)),
        name="expert_ffn",
    )(tile_expert, n_used, xs, g.reshape(1, d), wg, wu, wd)


def _combine_kernel(x_ref, p0_ref, p1_ref, p0n_ref, p1n_ref, gate_ref, gf_ref, ys_ref, o_ref, buf,
                    sem, *, final):
    i = pl.program_id(0)
    rows, d = x_ref.shape
    slot = i % 2

    def gather(pa_ref, pb_ref, to_slot):
        def start(grp, c):
            base = pl.multiple_of(grp * SUBLANES, SUBLANES)
            for u in range(SUBLANES):
                for k, p_ref in enumerate((pa_ref, pb_ref)):
                    pltpu.make_async_copy(ys_ref.at[pl.ds(p_ref[base + u], 1)],
                                          buf.at[to_slot, k, grp, pl.ds(u, 1)],
                                          sem.at[to_slot]).start(k)
            return c

        lax.fori_loop(0, rows // SUBLANES, start, 0)

    @pl.when(i == 0)
    def _():
        gather(p0_ref, p1_ref, 0)

    @pl.when(i + 1 < pl.num_programs(0))
    def _():
        gather(p0n_ref, p1n_ref, 1 - slot)

    for k in range(TOP_K):
        pltpu.make_async_copy(ys_ref.at[pl.ds(0, rows)], o_ref, sem.at[slot]).wait()
    gates = gate_ref[...]
    y0, y1 = (buf[slot, k].reshape(rows, d) for k in range(TOP_K))
    out = x_ref[...] + gates[:, 0:1] * y0 + gates[:, 1:2] * y1
    if final:
        out = _rmsnorm(out, gf_ref[...])
    o_ref[...] = out


def _combine(x, off_rows, pos0, pos1, gates, ys, g_final, final):
    n_rows, d = x.shape
    rows = MOVE_ROWS
    off = off_rows // rows
    n = n_rows // rows
    smem_rows = pl.BlockSpec((rows,), lambda i: (off + i,), memory_space=pltpu.SMEM)
    smem_next = pl.BlockSpec((rows,), lambda i: (off + jnp.minimum(i + 1, n - 1),),
                             memory_space=pltpu.SMEM)
    xspec = pl.BlockSpec((rows, d), lambda i: (i, 0))
    return pl.pallas_call(
        functools.partial(_combine_kernel, final=final),
        grid=(n,),
        in_specs=[xspec, smem_rows, smem_rows, smem_next, smem_next,
                  pl.BlockSpec((rows, LANES), lambda i: (off + i, 0)), _const_spec((1, d)),
                  pl.BlockSpec(memory_space=pl.ANY)],
        out_specs=xspec,
        out_shape=jax.ShapeDtypeStruct((n_rows, d), F32),
        scratch_shapes=[pltpu.VMEM((2, TOP_K, rows // SUBLANES, SUBLANES, d), F32),
                        pltpu.SemaphoreType.DMA((2,))],
        compiler_params=_params(("arbitrary",)),
        name="moe_combine",
    )(x, pos0, pos1, pos0, pos1, gates, g_final.reshape(1, d), ys)


def _moe_layer(xa, xb, g, rw, rb, wg, wu, wd, g_final, final):
    d = xa.shape[1]
    t = xa.shape[0] + xb.shape[0]
    n_exp = rw.shape[-1]
    tile = FFN_ROWS
    rw_pad = jnp.zeros((d, LANES), F32).at[:, :n_exp].set(rw)
    rb_pad = jnp.full((1, LANES), -jnp.inf, F32).at[0, :n_exp].set(rb)
    sel, gates, counts = _router(xa, xb, g, rw_pad, rb_pad)

    counts = counts[0, :n_exp].astype(jnp.int32)
    gpad = (counts + tile - 1) // tile * tile
    gend = jnp.cumsum(gpad)
    gstart = gend - gpad
    sel = sel.astype(jnp.int32)
    pos0 = gstart[sel[0]] + sel[2]
    pos1 = gstart[sel[1]] + sel[3]
    n_tiles = (TOP_K * t + n_exp * (tile - 1)) // tile
    tile_ids = jnp.arange(n_tiles, dtype=jnp.int32)
    tile_expert = jnp.minimum(
        jnp.sum((gend // tile)[None, :] <= tile_ids[:, None], axis=1), n_exp - 1).astype(jnp.int32)
    n_used = (gend[-1:] // tile).astype(jnp.int32)

    meta = jnp.concatenate([gend, gpad, n_used]).astype(jnp.int32)
    xs = _dispatch(xa, xb, pos0, pos1, meta, n_exp, n_tiles * tile, tile)
    ys = _expert_ffn(xs, g, tile_expert, n_used, wg, wu, wd, tile)
    return (_combine(xa, 0, pos0, pos1, gates, ys, g_final, final),
            _combine(xb, xa.shape[0], pos0, pos1, gates, ys, g_final, final))


def kernel(x_prompt, x_sample, cache_mem_k, cache_mem_v, state_pool, mem_prompt, norm_mix_g, w_in, pool_w, pool_scale, gm_ln_g, gm_ln_b, gm_ws, gm_bs, w_mix_out, norm_xa_g, norm_mem_g, xa_wq, xa_wk, xa_wv, xa_wo, norm_ffn_g, ffn_wg, ffn_wu, ffn_wd, moe_router_w, moe_router_b, moe_wg, moe_wu, moe_wd, norm_final_g):
    batch, seq, d = x_prompt.shape
    dec_batch, dec_seq, _ = x_sample.shape
    depth = norm_mix_g.shape[0]
    n_mem, n_mem_heads, mem_hd = cache_mem_k.shape[2:]
    d_pool = pool_scale.shape[-1]
    d_gate = gm_ln_g.shape[-1]
    rows_p, rows_s = batch * seq, dec_batch * dec_seq
    seg_per_tile = GATE_ROWS // dec_seq
    assert seq % MIX_ROWS == 0 and MIX_ROWS % GATE_ROWS == 0
    assert GATE_ROWS % GMLP_CHUNK == 0
    assert GATE_ROWS % dec_seq == 0 and dec_batch % seg_per_tile == 0 and dec_seq >= HIST_ROWS
    assert dec_seq <= GMLP_CHUNK and PAST_LEN % GMLP_CHUNK == 0
    assert dec_batch % ATTN_SAMPLE_STREAMS == 0
    assert rows_p % FFN_ROWS == 0 and rows_s % FFN_ROWS == 0
    assert rows_p % MOVE_ROWS == 0 and rows_s % MOVE_ROWS == 0

    w = dict(norm_mix_g=norm_mix_g, w_in=w_in.astype(BF16), pool_w=pool_w.astype(BF16),
             pool_scale=pool_scale, gm_ln_g=gm_ln_g, gm_ln_b=gm_ln_b,
             w_mix_out=w_mix_out.astype(BF16), norm_xa_g=norm_xa_g,
             xa_wq=xa_wq.astype(BF16), xa_wo=xa_wo.astype(BF16))

    mk, mv, new_mem_k, new_mem_v = _memory_kv(mem_prompt, norm_mem_g, xa_wk, xa_wv, n_mem_heads)
    pk, pv = mk.reshape(depth, batch, n_mem, d), mv.reshape(depth, batch, n_mem, d)

    hist_p0 = jnp.zeros((batch, HIST_ROWS, d_pool), F32)
    hist_s = jnp.pad(state_pool, ((0, 0), (0, 0), (HIST_ROWS - state_pool.shape[2], 0), (0, 0)))
    keep = HIST_ROWS - state_pool.shape[2]

    xp = x_prompt.reshape(rows_p, d)
    xs = x_sample.reshape(rows_s, d)
    pool_p, pool_s, v_rows = [], [], []
    n_exp, _, d_ff = moe_wg.shape[1:]
    expert_w = {}
    for l in range(depth):
        wbig_p, bias_p = _gate_operands(gm_ws[l], gm_bs[l], GATE_ROWS, GMLP_CHUNK)
        wbig_s, bias_s = _gate_operands(gm_ws[l], gm_bs[l], GATE_ROWS, dec_seq)
        if l % 2 == 1:
            mix_cast = [moe_wd[l // 2].reshape(n_exp * d_ff, d)]
        else:
            mix_cast = [ffn_wg[l // 2], ffn_wu[l // 2], ffn_wd[l // 2]]
        xp, hp, *mix_done = _mixer(xp, l > 0, batch, seq // MIX_ROWS, 1, MIX_ROWS, 0, hist_p0, l,
                                   w, wbig_p, bias_p, False, mix_cast, (pk, pv, n_mem_heads))
        if l % 2 == 1:
            expert_w["wd"] = mix_done[0].reshape(n_exp, d_ff, d)
        else:
            ffn_b = mix_done
        xs, hs, vr = _mixer(xs, l > 0, dec_batch // seg_per_tile, 1, seg_per_tile, dec_seq,
                            PAST_LEN, hist_s[l], l, w, wbig_s, bias_s, True)
        pool_p.append(hp[:, keep:])
        pool_s.append(hs[:, keep:])
        v_rows.append(vr.reshape(dec_batch, dec_seq, d_gate))

        xs = _attention_cached(xs, ATTN_SAMPLE_STREAMS, cache_mem_k, cache_mem_v, l, w)

        final = l == depth - 1
        if l % 2 == 0:
            ffn_cast = ([moe_wg[l // 2].reshape(n_exp * d, d_ff),
                         moe_wu[l // 2].reshape(n_exp * d, d_ff)] if l + 1 < depth else [])
            xp, *ffn_done = _dense_ffn(xp, norm_ffn_g[l], *ffn_b, norm_final_g, final, ffn_cast)
            xs, = _dense_ffn(xs, norm_ffn_g[l], *ffn_b, norm_final_g, final)
            if ffn_done:
                expert_w["wg"], expert_w["wu"] = (a.reshape(n_exp, d, d_ff) for a in ffn_done)
        else:
            i = l // 2
            xp, xs = _moe_layer(xp, xs, norm_ffn_g[l], moe_router_w[i], moe_router_b[i],
                                expert_w["wg"], expert_w["wu"], expert_w["wd"], norm_final_g, final)

    return (xp.reshape(batch, seq, d), xs.reshape(dec_batch, dec_seq, d),
            new_mem_k, new_mem_v, jnp.stack(pool_p), jnp.stack(pool_s), jnp.stack(v_rows))
```

```python
import functools

import jax
import jax.numpy as jnp
from jax import lax
from jax.experimental import pallas as pl
from jax.experimental.pallas import tpu as pltpu

EPS = 1e-6
PAST_LEN = 4096
POOL_WINDOWS = (2, 4, 8, 16)
HIST_ROWS = 16
GMLP_CHUNK = 128
LANES = 128
SUBLANES = 8
TOP_K = 2

V7X_VMEM_BYTES = 64 * 1024 * 1024
VMEM_LIMIT = V7X_VMEM_BYTES - 8 * 1024 * 1024

MIX_ROWS = 1024
GATE_ROWS = 256
ATTN_SAMPLE_STREAMS = 8
FFN_ROWS = 512
FF_CHUNK = 256
MOVE_ROWS = 1024

BF16 = jnp.bfloat16
F32 = jnp.float32


def _rmsnorm(x, g):
    return x * lax.rsqrt(jnp.mean(x * x, axis=-1, keepdims=True) + EPS) * g


def _dot(a, b):
    return jnp.dot(a, b, preferred_element_type=F32)


def _params(sem, vmem=VMEM_LIMIT):
    return pltpu.CompilerParams(dimension_semantics=sem, vmem_limit_bytes=vmem)


def _const_spec(shape, layer=None, single_buffer=False):
    nd = len(shape)
    mode = dict(pipeline_mode=pl.Buffered(1)) if single_buffer else {}
    if layer is None:
        return pl.BlockSpec(shape, lambda *_: (0,) * nd, **mode)
    return pl.BlockSpec((None,) + shape, lambda *_: (layer,) + (0,) * nd, **mode)


def _side_cast(arrays, n_steps, step_of):
    specs, shapes = [], []
    for a in arrays:
        rows, cols = a.shape
        n_blk = n_steps
        while rows % (n_blk * 2 * SUBLANES):
            n_blk -= 1
        imap = lambda *ids, n_blk=n_blk: (jnp.minimum(step_of(*ids), n_blk - 1), 0)
        specs.append(pl.BlockSpec((rows // n_blk, cols), imap))
        shapes.append(jax.ShapeDtypeStruct(a.shape, BF16))
    return specs, list(specs), shapes


def _run_side_cast(in_refs, out_refs):
    for src, dst in zip(in_refs, out_refs):
        dst[...] = src[...].astype(BF16)


def _memkv_kernel(mem_ref, g_ref, wk_ref, wv_ref, k_ref, v_ref, k5_ref, v5_ref, sem):
    l = pl.program_id(0)
    m = _rmsnorm(mem_ref[...], g_ref[...]).astype(BF16)
    k_ref[0] = _dot(m, wk_ref[...].astype(BF16))
    v_ref[0] = _dot(m, wv_ref[...].astype(BF16))
    depth, n_b, n_mem, n_heads, hd = k5_ref.shape
    for layer in range(depth):
        @pl.when(l == layer)
        def _():
            copies = [pltpu.make_async_copy(src.at[0, pl.ds(b * n_mem, n_mem), pl.ds(h * hd, hd)],
                                            dst.at[layer, b, :, h, :], sem)
                      for src, dst in ((k_ref, k5_ref), (v_ref, v5_ref))
                      for b in range(n_b) for h in range(n_heads)]
            for c in copies:
                c.start()
            for c in copies:
                c.wait()


def _memory_kv(mem, norm_g, wk, wv, n_heads):
    depth, d = norm_g.shape
    n_b, n_mem, _ = mem.shape
    rows = n_b * n_mem
    lay = lambda l: (l, 0, 0)
    five_d = jax.ShapeDtypeStruct((depth, n_b, n_mem, n_heads, d // n_heads), F32)
    return pl.pallas_call(
        _memkv_kernel,
        grid=(depth,),
        in_specs=[pl.BlockSpec((rows, d), lambda l: (0, 0)),
                  pl.BlockSpec((None, 1, d), lay),
                  pl.BlockSpec((None, d, d), lay),
                  pl.BlockSpec((None, d, d), lay)],
        out_specs=[pl.BlockSpec((1, rows, d), lay)] * 2
                  + [pl.BlockSpec(memory_space=pl.ANY)] * 2,
        out_shape=[jax.ShapeDtypeStruct((depth, rows, d), F32)] * 2 + [five_d] * 2,
        scratch_shapes=[pltpu.SemaphoreType.DMA],
        compiler_params=_params(("arbitrary",)),
        name="memory_kv",
    )(mem.reshape(rows, d), norm_g.reshape(depth, 1, d), wk, wv)


def _mixer_kernel(x_ref, g_ref, win_ref, poolw_ref, pscale_ref, lng_ref, lnb_ref, wbig_ref,
                  bias_ref, wout_ref, hist_ref, *rest, n_seg, seg_rows, pos0, keep_v, n_cast,
                  attn_heads):
    attn_refs, rest = (rest[:5], rest[5:]) if attn_heads else ((), rest)
    cast_in, rest = rest[:n_cast], rest[n_cast:]
    o_ref, hist_out_ref = rest[:2]
    v_ref = rest[2] if keep_v else None
    cast_out = rest[len(rest) - 1 - n_cast:-1]
    hist_sc = rest[-1]
    _run_side_cast(cast_in, cast_out)
    j = pl.program_id(1)
    d_pool = pscale_ref.shape[-1]
    d_gate = lng_ref.shape[-1]
    pg = d_pool // len(POOL_WINDOWS)
    n_heads = wbig_ref.shape[0]
    gh = d_gate // n_heads

    @pl.when(j == 0)
    def _():
        hist_sc[...] = hist_ref[...]

    x = x_ref[...]
    h = _rmsnorm(x, g_ref[...]).astype(BF16)
    z = _dot(h, win_ref[...])
    p = z[:, :d_pool]
    u = jax.nn.gelu(z[:, d_pool:d_pool + d_gate])
    vpre = jax.nn.gelu(z[:, d_pool + d_gate:])
    mu = jnp.mean(vpre, axis=-1, keepdims=True)
    vc = vpre - mu
    var = jnp.mean(vc * vc, axis=-1, keepdims=True)
    v = vc * lax.rsqrt(var + EPS) * lng_ref[...] + lnb_ref[...]
    if keep_v:
        v_ref[...] = v

    pos = pos0 + j * seg_rows + lax.broadcasted_iota(jnp.int32, (seg_rows, pg), 0)
    pd_segs = []
    for s in range(n_seg):
        ps = p[s * seg_rows:(s + 1) * seg_rows]
        ext = jnp.concatenate([hist_sc[s], ps], axis=0)
        hist_sc[s] = ps[seg_rows - HIST_ROWS:]
        cols = []
        for gi, w in enumerate(POOL_WINDOWS):
            acc = ext[:, gi * pg:(gi + 1) * pg]
            span = 1
            while span < w:
                acc = acc + pltpu.roll(acc, span, 0)
                span *= 2
            cnt = jnp.minimum(pos + 1, w).astype(F32)
            cols.append(acc[HIST_ROWS:] / cnt - ps[:, gi * pg:(gi + 1) * pg])
        pd_segs.append(jnp.concatenate(cols, axis=1))
    pd = pd_segs[0] if n_seg == 1 else jnp.concatenate(pd_segs, axis=0)
    hist_out_ref[...] = hist_sc[...]

    pdb = pd.astype(BF16)
    parts = []
    for gi in range(len(POOL_WINDOWS)):
        sl = slice(gi * pg, (gi + 1) * pg)
        parts.append(_dot(pdb[:, sl], poolw_ref[gi]) * pscale_ref[:, sl])
    vb = v.astype(BF16)
    gate_rows = wbig_ref.shape[1]
    for hi in range(n_heads):
        sl = slice(hi * gh, (hi + 1) * gh)
        mixed = [_dot(wbig_ref[hi], vb[r0:r0 + gate_rows, sl]) + bias_ref[:, sl]
                 for r0 in range(0, x.shape[0], gate_rows)]
        mixed = mixed[0] if len(mixed) == 1 else jnp.concatenate(mixed, axis=0)
        parts.append(u[:, sl] * mixed)
    cat = jnp.concatenate(parts, axis=1).astype(BF16)
    x = x + _dot(cat, wout_ref[...])
    if attn_heads:
        ga_ref, wq_ref, wo_ref, k_ref, v_ref2 = attn_refs
        hd = x.shape[-1] // attn_heads
        q = _queries(x, ga_ref, wq_ref, hd)
        kb = k_ref[...].astype(BF16)
        vb2 = v_ref2[...].astype(BF16)
        o = _attend(q, lambda hi: kb[:, hi * hd:(hi + 1) * hd],
                    lambda hi: vb2[:, hi * hd:(hi + 1) * hd], attn_heads, hd)
        x = x + _dot(o.astype(BF16), wo_ref[...])
    o_ref[...] = x


def _mixer(x, in_place, n_b, n_j, n_seg, seg_rows, pos0, hist, layer, w, wbig, bias, keep_v,
           cast=(), attn=None):
    d = x.shape[1]
    rows = n_seg * seg_rows
    d_in = w["w_in"].shape[-1]
    d_pool = w["pool_scale"].shape[-1]
    d_gate = w["gm_ln_g"].shape[-1]
    vec = lambda a: a[layer].reshape(1, -1)
    xspec = pl.BlockSpec((rows, d), lambda b, j: (b * n_j + j, 0))
    hspec = pl.BlockSpec((n_seg, HIST_ROWS, d_pool), lambda b, j: (b, 0, 0))
    in_specs = [xspec, _const_spec((1, d)), _const_spec((d, d_in), layer),
                _const_spec(w["pool_w"].shape[1:], layer),
                _const_spec((1, d_pool)), _const_spec((1, d_gate)), _const_spec((1, d_gate)),
                _const_spec(wbig.shape), _const_spec(bias.shape), _const_spec((d, d), layer),
                hspec]
    out_specs = [xspec, hspec]
    out_shape = [jax.ShapeDtypeStruct(x.shape, F32), jax.ShapeDtypeStruct(hist.shape, F32)]
    if keep_v:
        out_specs.append(pl.BlockSpec((rows, d_gate), lambda b, j: (b * n_j + j, 0)))
        out_shape.append(jax.ShapeDtypeStruct((x.shape[0], d_gate), F32))
    c_in, c_out, c_shapes = _side_cast(cast, n_b * n_j, lambda b, j: b * n_j + j)
    attn_args, attn_heads = [], 0
    if attn is not None:
        mem_k, mem_v, attn_heads = attn
        kvspec = pl.BlockSpec((None, None, mem_k.shape[2], d), lambda b, j: (layer, b, 0, 0))
        in_specs += [_const_spec((1, d)), _const_spec((d, d), layer), _const_spec((d, d), layer),
                     kvspec, kvspec]
        attn_args = [vec(w["norm_xa_g"]), w["xa_wq"], w["xa_wo"], mem_k, mem_v]
    return pl.pallas_call(
        functools.partial(_mixer_kernel, n_seg=n_seg, seg_rows=seg_rows, pos0=pos0, keep_v=keep_v,
                          n_cast=len(cast), attn_heads=attn_heads),
        grid=(n_b, n_j),
        in_specs=in_specs + c_in,
        out_specs=out_specs + c_out,
        out_shape=out_shape + c_shapes,
        scratch_shapes=[pltpu.VMEM((n_seg, HIST_ROWS, d_pool), F32)],
        input_output_aliases={0: 0} if in_place else {},
        compiler_params=_params(("arbitrary", "arbitrary")),
        name="mixer_attention" if attn_heads else "token_mixer",
    )(x, vec(w["norm_mix_g"]), w["w_in"], w["pool_w"], vec(w["pool_scale"]),
      vec(w["gm_ln_g"]), vec(w["gm_ln_b"]), wbig, bias, w["w_mix_out"], hist, *attn_args, *cast)


def _gate_operands(ws, bs, rows, chunk):
    n_heads = ws.shape[0]
    gh = GMLP_CHUNK
    tri = jnp.tril(jnp.ones((chunk, chunk), dtype=bool))
    wc = jnp.where(tri[None], ws[:, :chunk, :chunk], 0)
    eye = jnp.eye(rows // chunk, dtype=ws.dtype)
    wbig = jnp.einsum("ab,hts->hatbs", eye, wc).reshape(n_heads, rows, rows).astype(BF16)
    bias = jnp.tile(bs[:, :chunk].T, (rows // chunk, 1))
    bias = jnp.repeat(bias, gh, axis=1)
    return wbig, bias


def _queries(x, g_ref, wq_ref, hd):
    h = _rmsnorm(x, g_ref[...]).astype(BF16)
    return (_dot(h, wq_ref[...]) * (hd ** -0.5)).astype(BF16)


def _attend(qs, k_head, v_head, n_heads, hd):
    outs = []
    for hi in range(n_heads):
        s = lax.dot_general(qs[:, hi * hd:(hi + 1) * hd], k_head(hi), (((1,), (1,)), ((), ())),
                            preferred_element_type=F32)
        e = jnp.exp(s - jnp.max(s, axis=-1, keepdims=True))
        prob = e * (1.0 / jnp.sum(e, axis=-1, keepdims=True))
        outs.append(_dot(prob.astype(BF16), v_head(hi)))
    return jnp.concatenate(outs, axis=1)


def _attn_cached_kernel(x_ref, g_ref, wq_ref, wo_ref, k_hbm, v_hbm, o_ref, kbuf, vbuf, sem, *,
                        layer, streams):
    i = pl.program_id(0)
    n_heads, hd = k_hbm.shape[-2:]
    slot = i % 2

    def copies(step, to_slot):
        src = pl.ds(step * streams, streams)
        return [pltpu.make_async_copy(hbm.at[layer, src, :, hi, :], buf.at[to_slot, hi],
                                      sem.at[to_slot])
                for hbm, buf in ((k_hbm, kbuf), (v_hbm, vbuf)) for hi in range(n_heads)]

    @pl.when(i == 0)
    def _():
        for c in copies(0, 0):
            c.start()

    @pl.when(i + 1 < pl.num_programs(0))
    def _():
        for c in copies(i + 1, 1 - slot):
            c.start()

    x = x_ref[...]
    q = _queries(x, g_ref, wq_ref, hd)
    for c in copies(i, slot):
        c.wait()
    seg = x.shape[0] // streams
    outs = []
    for hi in range(n_heads):
        qh = q[:, hi * hd:(hi + 1) * hd].reshape(streams, seg, hd)
        s = jnp.einsum("sld,smd->slm", qh, kbuf[slot, hi].astype(BF16),
                       preferred_element_type=F32)
        e = jnp.exp(s - jnp.max(s, axis=-1, keepdims=True))
        prob = e * (1.0 / jnp.sum(e, axis=-1, keepdims=True))
        oh = jnp.einsum("slm,smd->sld", prob.astype(BF16), vbuf[slot, hi].astype(BF16),
                        preferred_element_type=F32)
        outs.append(oh.reshape(streams * seg, hd))
    o = jnp.concatenate(outs, axis=1).astype(BF16)
    o_ref[...] = x + _dot(o, wo_ref[...])


def _attention_cached(x, streams, cache_k, cache_v, layer, w):
    d = x.shape[1]
    n_streams, n_mem, n_heads, hd = cache_k.shape[1:]
    rows = x.shape[0] // n_streams * streams
    xspec = pl.BlockSpec((rows, d), lambda i: (i, 0))
    hbm = pl.BlockSpec(memory_space=pl.ANY)
    buf = pltpu.VMEM((2, n_heads, streams, n_mem, hd), F32)
    return pl.pallas_call(
        functools.partial(_attn_cached_kernel, layer=layer, streams=streams),
        grid=(n_streams // streams,),
        in_specs=[xspec, _const_spec((1, d)), _const_spec((d, d), layer),
                  _const_spec((d, d), layer), hbm, hbm],
        out_specs=xspec,
        out_shape=jax.ShapeDtypeStruct(x.shape, F32),
        scratch_shapes=[buf, buf, pltpu.SemaphoreType.DMA((2,))],
        input_output_aliases={0: 0},
        compiler_params=_params(("arbitrary",)),
        name="cached_attention",
    )(x, w["norm_xa_g"][layer].reshape(1, d), w["xa_wq"], w["xa_wo"], cache_k, cache_v)


def _swiglu_into(o_ref, hb, wg_ref, wu_ref, wd_ref, base):
    d_ff = wg_ref.shape[-1]
    for c in range(d_ff // FF_CHUNK):
        sl = slice(c * FF_CHUNK, (c + 1) * FF_CHUNK)
        a = (jax.nn.silu(_dot(hb, wg_ref[:, sl])) * _dot(hb, wu_ref[:, sl])).astype(BF16)
        part = _dot(a, wd_ref[sl, :])
        if c == 0:
            o_ref[...] = part if base is None else base + part
        else:
            o_ref[...] += part


def _ffn_kernel(x_ref, g_ref, wg_ref, wu_ref, wd_ref, gf_ref, *rest, final, n_cast):
    cast_in, o_ref, cast_out = rest[:n_cast], rest[n_cast], rest[n_cast + 1:]
    _run_side_cast(cast_in, cast_out)
    x = x_ref[...]
    hb = _rmsnorm(x, g_ref[...]).astype(BF16)
    _swiglu_into(o_ref, hb, wg_ref, wu_ref, wd_ref, x)
    if final:
        o_ref[...] = _rmsnorm(o_ref[...], gf_ref[...])


def _dense_ffn(x, g, wg, wu, wd, g_final, final, cast=()):
    t, d = x.shape
    d_ff = wg.shape[-1]
    assert t % FFN_ROWS == 0 and d_ff % FF_CHUNK == 0
    n = t // FFN_ROWS
    xspec = pl.BlockSpec((FFN_ROWS, d), lambda i: (i, 0))
    c_in, c_out, c_shapes = _side_cast(cast, n, lambda i: i)
    wspec = lambda shape: _const_spec(shape, single_buffer=True)
    return pl.pallas_call(
        functools.partial(_ffn_kernel, final=final, n_cast=len(cast)),
        grid=(n,),
        in_specs=[xspec, _const_spec((1, d)), wspec((d, d_ff)), wspec((d, d_ff)),
                  wspec((d_ff, d)), _const_spec((1, d))] + c_in,
        out_specs=[xspec] + c_out,
        out_shape=[jax.ShapeDtypeStruct((t, d), F32)] + c_shapes,
        input_output_aliases={0: 0},
        compiler_params=_params(("arbitrary",)),
        name="dense_ffn",
    )(x, g.reshape(1, d), wg, wu, wd, g_final.reshape(1, d), *cast)


def _router_kernel(x_ref, g_ref, rw_ref, rb_ref, before_ref, cnt0_ref, sel_ref, gate_ref, cnt_ref,
                   carry):
    i = pl.program_id(0)

    @pl.when(i == 0)
    def _():
        carry[...] = cnt0_ref[...]

    h = _rmsnorm(x_ref[...], g_ref[...])
    h_hi = h.astype(BF16)
    h_lo = (h - h_hi.astype(F32)).astype(BF16)
    by_hi = _dot(h_hi, rw_ref[...])
    logits = by_hi[:, :LANES] + (by_hi[:, LANES:] + _dot(h_lo, rw_ref[:, :LANES])) + rb_ref[...]
    lane_i = lax.broadcasted_iota(jnp.int32, logits.shape, 1)
    lane = lane_i.astype(F32)
    m0 = jnp.max(logits, axis=-1, keepdims=True)
    e0 = jnp.min(jnp.where(logits == m0, lane, float(LANES)), axis=-1, keepdims=True)
    rest = jnp.where(lane == e0, -jnp.inf, logits)
    m1 = jnp.max(rest, axis=-1, keepdims=True)
    e1 = jnp.min(jnp.where(rest == m1, lane, float(LANES)), axis=-1, keepdims=True)
    t = jnp.exp(m1 - m0)
    g0 = 1.0 / (1.0 + t)
    g1 = t * g0
    hot0 = (lane == e0).astype(F32)
    hot1 = (lane == e1).astype(F32)
    both = hot0 + hot1
    prior = _dot(before_ref[...], both.astype(BF16)) + carry[...]
    rank0 = jnp.sum(hot0 * prior, axis=-1, keepdims=True)
    rank1 = jnp.sum(hot1 * prior, axis=-1, keepdims=True)
    carry[...] += jnp.sum(both, axis=0, keepdims=True)
    cnt_ref[...] = carry[...]
    sel = jnp.where(lane_i == 0, e0, jnp.where(lane_i == 1, e1,
                    jnp.where(lane_i == 2, rank0, jnp.where(lane_i == 3, rank1, 0.0))))
    sel_ref[...] = jnp.transpose(sel)[:SUBLANES]
    gate_ref[...] = jnp.where(lane_i == 0, g0, jnp.where(lane_i == 1, g1, 0.0))


def _router(x, g, rw_parts, rb_pad, counts_before):
    t, d = x.shape
    rows = FFN_ROWS
    before = jnp.tril(jnp.ones((rows, rows), F32), -1).astype(BF16)
    return pl.pallas_call(
        _router_kernel,
        grid=(t // rows,),
        in_specs=[pl.BlockSpec((rows, d), lambda i: (i, 0)), _const_spec((1, d)),
                  _const_spec((d, 2 * LANES)), _const_spec((1, LANES)), _const_spec((rows, rows)),
                  _const_spec((1, LANES))],
        out_specs=[pl.BlockSpec((SUBLANES, rows), lambda i: (0, i)),
                   pl.BlockSpec((rows, LANES), lambda i: (i, 0)), _const_spec((1, LANES))],
        out_shape=[jax.ShapeDtypeStruct((SUBLANES, t), F32),
                   jax.ShapeDtypeStruct((t, LANES), F32),
                   jax.ShapeDtypeStruct((1, LANES), F32)],
        scratch_shapes=[pltpu.VMEM((1, LANES), F32)],
        compiler_params=_params(("arbitrary",)),
        name="moe_router",
    )(x, g.reshape(1, d), rw_parts, rb_pad, before, counts_before)


def _dispatch_kernel(meta_ref, xa_ref, xb_ref, p0_ref, p1_ref, xs_ref, zero_sc, sem, zsem, *,
                     n_a, n_exp, min_tiles):
    i = pl.program_id(0)
    rows = xa_ref.shape[0] * SUBLANES
    tile = zero_sc.shape[0]
    n_tiles = xs_ref.shape[0] // tile

    def zero_copy(row_end):
        start = pl.multiple_of(row_end - tile, tile)
        return pltpu.make_async_copy(zero_sc, xs_ref.at[pl.ds(start, tile)], zsem)

    fills = [(meta_ref[n_exp + e] > 0, meta_ref[e]) for e in range(n_exp)]
    fills += [(k >= meta_ref[2 * n_exp], (k + 1) * tile) for k in range(min_tiles, n_tiles)]

    @pl.when(i == 0)
    def _():
        zero_sc[...] = jnp.zeros_like(zero_sc)
        for needed, row_end in fills:
            @pl.when(needed)
            def _():
                zero_copy(row_end).start()
        for needed, row_end in fills:
            @pl.when(needed)
            def _():
                zero_copy(row_end).wait()

    def scatter_rows(x_ref):
        def start(grp, c):
            base = pl.multiple_of(grp * SUBLANES, SUBLANES)
            for u in range(SUBLANES):
                src = x_ref.at[grp, pl.ds(u, 1)]
                pltpu.make_async_copy(src, xs_ref.at[pl.ds(p0_ref[base + u], 1)], sem).start(0)
                pltpu.make_async_copy(src, xs_ref.at[pl.ds(p1_ref[base + u], 1)], sem).start(1)
            return c

        lax.fori_loop(0, rows // SUBLANES, start, 0)
        for _ in range(TOP_K * rows // tile):
            pltpu.make_async_copy(zero_sc, xs_ref.at[pl.ds(0, tile)], sem).wait()

    @pl.when(i < n_a)
    def _():
        scatter_rows(xa_ref)

    @pl.when(i >= n_a)
    def _():
        scatter_rows(xb_ref)


def _dispatch(xa, xb, pos0, pos1, meta, n_exp, n_sorted, tile):
    d = xa.shape[1]
    rows = MOVE_ROWS
    n_a, n_b = xa.shape[0] // rows, xb.shape[0] // rows
    t = xa.shape[0] + xb.shape[0]
    smem_rows = pl.BlockSpec((rows,), lambda i, *_: (i,), memory_space=pltpu.SMEM)
    min_tiles = -(-TOP_K * t // tile)
    assert rows % tile == 0
    grp = rows // SUBLANES
    grouped = lambda x: x.reshape(x.shape[0] // SUBLANES, SUBLANES, d)
    x_specs = [pl.BlockSpec((grp, SUBLANES, d), lambda i, *_: (jnp.minimum(i, n_a - 1), 0, 0)),
               pl.BlockSpec((grp, SUBLANES, d), lambda i, *_: (jnp.maximum(i - n_a, 0), 0, 0))]
    return pl.pallas_call(
        functools.partial(_dispatch_kernel, n_a=n_a, n_exp=n_exp, min_tiles=min_tiles),
        grid_spec=pltpu.PrefetchScalarGridSpec(
            num_scalar_prefetch=1,
            grid=(n_a + n_b,),
            in_specs=x_specs + [smem_rows, smem_rows],
            out_specs=pl.BlockSpec(memory_space=pl.ANY),
            scratch_shapes=[pltpu.VMEM((tile, d), F32), pltpu.SemaphoreType.DMA,
                            pltpu.SemaphoreType.DMA]),
        out_shape=jax.ShapeDtypeStruct((n_sorted, d), F32),
        compiler_params=_params(("arbitrary",)),
        name="moe_dispatch",
    )(meta, grouped(xa), grouped(xb), pos0, pos1)


def _expert_kernel(te_ref, nu_ref, xs_ref, g_ref, wg_ref, wu_ref, wd_ref, ys_ref):
    used = pl.program_id(0) < nu_ref[0]

    @pl.when(used)
    def _():
        hb = _rmsnorm(xs_ref[...], g_ref[...]).astype(BF16)
        _swiglu_into(ys_ref, hb, wg_ref, wu_ref, wd_ref, None)

    @pl.when(jnp.logical_not(used))
    def _():
        ys_ref[...] = jnp.zeros_like(ys_ref)


def _expert_ffn(xs, g, tile_expert, n_used, wg, wu, wd, tile):
    n_sorted, d = xs.shape
    d_ff = wg.shape[-1]
    rowmap = lambda i, te, nu: (jnp.minimum(i, nu[0] - 1), 0)
    wmap = lambda i, te, nu: (te[i], 0, 0)
    return pl.pallas_call(
        _expert_kernel,
        grid_spec=pltpu.PrefetchScalarGridSpec(
            num_scalar_prefetch=2,
            grid=(n_sorted // tile,),
            in_specs=[pl.BlockSpec((tile, d), rowmap),
                      pl.BlockSpec((1, d), lambda i, te, nu: (0, 0)),
                      pl.BlockSpec((None, d, d_ff), wmap),
                      pl.BlockSpec((None, d, d_ff), wmap),
                      pl.BlockSpec((None, d_ff, d), wmap)],
            out_specs=pl.BlockSpec((tile, d), lambda i, te, nu: (i, 0))),
        out_shape=jax.ShapeDtypeStruct((n_sorted, d), F32),
        compiler_params=_params(("arbitrary",)),
        name="expert_ffn",
    )(tile_expert, n_used, xs, g.reshape(1, d), wg, wu, wd)


def _combine_kernel(x_ref, p0_ref, p1_ref, p0n_ref, p1n_ref, gate_ref, gf_ref, ys_ref, o_ref, buf,
                    sem, *, final):
    i = pl.program_id(0)
    rows, d = x_ref.shape
    slot = i % 2

    def gather(pa_ref, pb_ref, to_slot):
        def start(grp, c):
            base = pl.multiple_of(grp * SUBLANES, SUBLANES)
            for u in range(SUBLANES):
                for k, p_ref in enumerate((pa_ref, pb_ref)):
                    pltpu.make_async_copy(ys_ref.at[pl.ds(p_ref[base + u], 1)],
                                          buf.at[to_slot, k, grp, pl.ds(u, 1)],
                                          sem.at[to_slot]).start(k)
            return c

        lax.fori_loop(0, rows // SUBLANES, start, 0)

    @pl.when(i == 0)
    def _():
        gather(p0_ref, p1_ref, 0)

    @pl.when(i + 1 < pl.num_programs(0))
    def _():
        gather(p0n_ref, p1n_ref, 1 - slot)

    for k in range(TOP_K):
        pltpu.make_async_copy(ys_ref.at[pl.ds(0, rows)], o_ref, sem.at[slot]).wait()
    gates = gate_ref[...]
    y0, y1 = (buf[slot, k].reshape(rows, d) for k in range(TOP_K))
    out = x_ref[...] + gates[:, 0:1] * y0 + gates[:, 1:2] * y1
    if final:
        out = _rmsnorm(out, gf_ref[...])
    o_ref[...] = out


def _combine(x, off_rows, pos0, pos1, gates, ys, g_final, final):
    n_rows, d = x.shape
    rows = MOVE_ROWS
    off = off_rows // rows
    n = n_rows // rows
    smem_rows = pl.BlockSpec((rows,), lambda i: (off + i,), memory_space=pltpu.SMEM)
    smem_next = pl.BlockSpec((rows,), lambda i: (off + jnp.minimum(i + 1, n - 1),),
                             memory_space=pltpu.SMEM)
    xspec = pl.BlockSpec((rows, d), lambda i: (i, 0))
    return pl.pallas_call(
        functools.partial(_combine_kernel, final=final),
        grid=(n,),
        in_specs=[xspec, smem_rows, smem_rows, smem_next, smem_next,
                  pl.BlockSpec((rows, LANES), lambda i: (i, 0)), _const_spec((1, d)),
                  pl.BlockSpec(memory_space=pl.ANY)],
        out_specs=xspec,
        out_shape=jax.ShapeDtypeStruct((n_rows, d), F32),
        scratch_shapes=[pltpu.VMEM((2, TOP_K, rows // SUBLANES, SUBLANES, d), F32),
                        pltpu.SemaphoreType.DMA((2,))],
        compiler_params=_params(("arbitrary",)),
        name="moe_combine",
    )(x, pos0, pos1, pos0, pos1, gates, g_final.reshape(1, d), ys)


def _moe_layer(xa, xb, g, rw, rb, wg, wu, wd, g_final, final):
    d = xa.shape[1]
    t = xa.shape[0] + xb.shape[0]
    n_exp = rw.shape[-1]
    tile = FFN_ROWS
    rw_pad = jnp.zeros((d, LANES), F32).at[:, :n_exp].set(rw)
    rw_hi = rw_pad.astype(BF16)
    rw_parts = jnp.concatenate([rw_hi, (rw_pad - rw_hi.astype(F32)).astype(BF16)], axis=1)
    rb_pad = jnp.full((1, LANES), -jnp.inf, F32).at[0, :n_exp].set(rb)
    sel_a, gates_a, counts_a = _router(xa, g, rw_parts, rb_pad, jnp.zeros((1, LANES), F32))
    sel_b, gates_b, counts = _router(xb, g, rw_parts, rb_pad, counts_a)
    sel = jnp.concatenate([sel_a, sel_b], axis=1)

    counts = counts[0, :n_exp].astype(jnp.int32)
    gpad = (counts + tile - 1) // tile * tile
    gend = jnp.cumsum(gpad)
    gstart = gend - gpad
    sel = sel.astype(jnp.int32)
    pos0 = gstart[sel[0]] + sel[2]
    pos1 = gstart[sel[1]] + sel[3]
    n_tiles = (TOP_K * t + n_exp * (tile - 1)) // tile
    tile_ids = jnp.arange(n_tiles, dtype=jnp.int32)
    tile_expert = jnp.minimum(
        jnp.sum((gend // tile)[None, :] <= tile_ids[:, None], axis=1), n_exp - 1).astype(jnp.int32)
    n_used = (gend[-1:] // tile).astype(jnp.int32)

    meta = jnp.concatenate([gend, gpad, n_used]).astype(jnp.int32)
    xs = _dispatch(xa, xb, pos0, pos1, meta, n_exp, n_tiles * tile, tile)
    ys = _expert_ffn(xs, g, tile_expert, n_used, wg, wu, wd, tile)
    return (_combine(xa, 0, pos0, pos1, gates_a, ys, g_final, final),
            _combine(xb, xa.shape[0], pos0, pos1, gates_b, ys, g_final, final))


def kernel(x_prompt, x_sample, cache_mem_k, cache_mem_v, state_pool, mem_prompt, norm_mix_g, w_in, pool_w, pool_scale, gm_ln_g, gm_ln_b, gm_ws, gm_bs, w_mix_out, norm_xa_g, norm_mem_g, xa_wq, xa_wk, xa_wv, xa_wo, norm_ffn_g, ffn_wg, ffn_wu, ffn_wd, moe_router_w, moe_router_b, moe_wg, moe_wu, moe_wd, norm_final_g):
    batch, seq, d = x_prompt.shape
    dec_batch, dec_seq, _ = x_sample.shape
    depth = norm_mix_g.shape[0]
    n_mem, n_mem_heads, mem_hd = cache_mem_k.shape[2:]
    d_pool = pool_scale.shape[-1]
    d_gate = gm_ln_g.shape[-1]
    rows_p, rows_s = batch * seq, dec_batch * dec_seq
    seg_per_tile = GATE_ROWS // dec_seq
    assert seq % MIX_ROWS == 0 and MIX_ROWS % GATE_ROWS == 0
    assert GATE_ROWS % GMLP_CHUNK == 0
    assert GATE_ROWS % dec_seq == 0 and dec_batch % seg_per_tile == 0 and dec_seq >= HIST_ROWS
    assert dec_seq <= GMLP_CHUNK and PAST_LEN % GMLP_CHUNK == 0
    assert dec_batch % ATTN_SAMPLE_STREAMS == 0
    assert rows_p % FFN_ROWS == 0 and rows_s % FFN_ROWS == 0
    assert rows_p % MOVE_ROWS == 0 and rows_s % MOVE_ROWS == 0

    w = dict(norm_mix_g=norm_mix_g, w_in=w_in.astype(BF16), pool_w=pool_w.astype(BF16),
             pool_scale=pool_scale, gm_ln_g=gm_ln_g, gm_ln_b=gm_ln_b,
             w_mix_out=w_mix_out.astype(BF16), norm_xa_g=norm_xa_g,
             xa_wq=xa_wq.astype(BF16), xa_wo=xa_wo.astype(BF16))

    mk, mv, new_mem_k, new_mem_v = _memory_kv(mem_prompt, norm_mem_g, xa_wk, xa_wv, n_mem_heads)
    pk, pv = mk.reshape(depth, batch, n_mem, d), mv.reshape(depth, batch, n_mem, d)

    hist_p0 = jnp.zeros((batch, HIST_ROWS, d_pool), F32)
    hist_s = jnp.pad(state_pool, ((0, 0), (0, 0), (HIST_ROWS - state_pool.shape[2], 0), (0, 0)))
    keep = HIST_ROWS - state_pool.shape[2]

    xp = x_prompt.reshape(rows_p, d)
    xs = x_sample.reshape(rows_s, d)
    pool_p, pool_s, v_rows = [], [], []
    n_exp, _, d_ff = moe_wg.shape[1:]
    expert_w = {}
    for l in range(depth):
        wbig_p, bias_p = _gate_operands(gm_ws[l], gm_bs[l], GATE_ROWS, GMLP_CHUNK)
        wbig_s, bias_s = _gate_operands(gm_ws[l], gm_bs[l], GATE_ROWS, dec_seq)
        if l % 2 == 1:
            mix_cast = [moe_wd[l // 2].reshape(n_exp * d_ff, d)]
        else:
            mix_cast = [ffn_wg[l // 2], ffn_wu[l // 2], ffn_wd[l // 2]]
        xp, hp, *mix_done = _mixer(xp, l > 0, batch, seq // MIX_ROWS, 1, MIX_ROWS, 0, hist_p0, l,
                                   w, wbig_p, bias_p, False, mix_cast, (pk, pv, n_mem_heads))
        if l % 2 == 1:
            expert_w["wd"] = mix_done[0].reshape(n_exp, d_ff, d)
        else:
            ffn_b = mix_done
        xs, hs, vr = _mixer(xs, l > 0, dec_batch // seg_per_tile, 1, seg_per_tile, dec_seq,
                            PAST_LEN, hist_s[l], l, w, wbig_s, bias_s, True)
        pool_p.append(hp[:, keep:])
        pool_s.append(hs[:, keep:])
        v_rows.append(vr.reshape(dec_batch, dec_seq, d_gate))

        xs = _attention_cached(xs, ATTN_SAMPLE_STREAMS, cache_mem_k, cache_mem_v, l, w)

        final = l == depth - 1
        if l % 2 == 0:
            ffn_cast = ([moe_wg[l // 2].reshape(n_exp * d, d_ff),
                         moe_wu[l // 2].reshape(n_exp * d, d_ff)] if l + 1 < depth else [])
            xp, *ffn_done = _dense_ffn(xp, norm_ffn_g[l], *ffn_b, norm_final_g, final, ffn_cast)
            xs, = _dense_ffn(xs, norm_ffn_g[l], *ffn_b, norm_final_g, final)
            if ffn_done:
                expert_w["wg"], expert_w["wu"] = (a.reshape(n_exp, d, d_ff) for a in ffn_done)
        else:
            i = l // 2
            xp, xs = _moe_layer(xp, xs, norm_ffn_g[l], moe_router_w[i], moe_router_b[i],
                                expert_w["wg"], expert_w["wu"], expert_w["wd"], norm_final_g, final)

    return (xp.reshape(batch, seq, d), xs.reshape(dec_batch, dec_seq, d),
            new_mem_k, new_mem_v, jnp.stack(pool_p), jnp.stack(pool_s), jnp.stack(v_rows))
```

```python
import functools

import jax
import jax.numpy as jnp
from jax import lax
from jax.experimental import pallas as pl
from jax.experimental.pallas import tpu as pltpu

EPS = 1e-6
PAST_LEN = 4096
POOL_WINDOWS = (2, 4, 8, 16)
HIST_ROWS = 16
GMLP_CHUNK = 128
LANES = 128
SUBLANES = 8
TOP_K = 2

V7X_VMEM_BYTES = 64 * 1024 * 1024
VMEM_LIMIT = V7X_VMEM_BYTES - 8 * 1024 * 1024

MIX_ROWS = 1024
GATE_ROWS = 256
ATTN_SAMPLE_STREAMS = 8
FFN_ROWS = 512
FF_CHUNK = 256
DISPATCH_ROWS = 1024
COMBINE_ROWS = 512

BF16 = jnp.bfloat16
F32 = jnp.float32


def _rmsnorm(x, g):
    return x * lax.rsqrt(jnp.mean(x * x, axis=-1, keepdims=True) + EPS) * g


def _dot(a, b):
    return jnp.dot(a, b, preferred_element_type=F32)


def _params(sem, vmem=VMEM_LIMIT):
    return pltpu.CompilerParams(dimension_semantics=sem, vmem_limit_bytes=vmem)


def _const_spec(shape, layer=None, single_buffer=False):
    nd = len(shape)
    mode = dict(pipeline_mode=pl.Buffered(1)) if single_buffer else {}
    if layer is None:
        return pl.BlockSpec(shape, lambda *_: (0,) * nd, **mode)
    return pl.BlockSpec((None,) + shape, lambda *_: (layer,) + (0,) * nd, **mode)


def _side_cast(arrays, n_steps, step_of):
    specs, shapes = [], []
    for a in arrays:
        rows, cols = a.shape
        n_blk = n_steps
        while rows % (n_blk * 2 * SUBLANES):
            n_blk -= 1
        imap = lambda *ids, n_blk=n_blk: (jnp.minimum(step_of(*ids), n_blk - 1), 0)
        specs.append(pl.BlockSpec((rows // n_blk, cols), imap))
        shapes.append(jax.ShapeDtypeStruct(a.shape, BF16))
    return specs, list(specs), shapes


def _run_side_cast(in_refs, out_refs):
    for src, dst in zip(in_refs, out_refs):
        dst[...] = src[...].astype(BF16)


def _memkv_kernel(mem_ref, g_ref, wk_ref, wv_ref, k_ref, v_ref, k5_ref, v5_ref, sem):
    l = pl.program_id(0)
    m = _rmsnorm(mem_ref[...], g_ref[...]).astype(BF16)
    k_ref[0] = _dot(m, wk_ref[...].astype(BF16))
    v_ref[0] = _dot(m, wv_ref[...].astype(BF16))
    depth, n_b, n_mem, n_heads, hd = k5_ref.shape
    for layer in range(depth):
        @pl.when(l == layer)
        def _():
            copies = [pltpu.make_async_copy(src.at[0, pl.ds(b * n_mem, n_mem), pl.ds(h * hd, hd)],
                                            dst.at[layer, b, :, h, :], sem)
                      for src, dst in ((k_ref, k5_ref), (v_ref, v5_ref))
                      for b in range(n_b) for h in range(n_heads)]
            for c in copies:
                c.start()
            for c in copies:
                c.wait()


def _memory_kv(mem, norm_g, wk, wv, n_heads):
    depth, d = norm_g.shape
    n_b, n_mem, _ = mem.shape
    rows = n_b * n_mem
    lay = lambda l: (l, 0, 0)
    five_d = jax.ShapeDtypeStruct((depth, n_b, n_mem, n_heads, d // n_heads), F32)
    return pl.pallas_call(
        _memkv_kernel,
        grid=(depth,),
        in_specs=[pl.BlockSpec((rows, d), lambda l: (0, 0)),
                  pl.BlockSpec((None, 1, d), lay),
                  pl.BlockSpec((None, d, d), lay),
                  pl.BlockSpec((None, d, d), lay)],
        out_specs=[pl.BlockSpec((1, rows, d), lay)] * 2
                  + [pl.BlockSpec(memory_space=pl.ANY)] * 2,
        out_shape=[jax.ShapeDtypeStruct((depth, rows, d), F32)] * 2 + [five_d] * 2,
        scratch_shapes=[pltpu.SemaphoreType.DMA],
        compiler_params=_params(("arbitrary",)),
        name="memory_kv",
    )(mem.reshape(rows, d), norm_g.reshape(depth, 1, d), wk, wv)


def _mixer_kernel(x_ref, g_ref, win_ref, poolw_ref, pscale_ref, lng_ref, lnb_ref, wbig_ref,
                  bias_ref, wout_ref, hist_ref, *rest, n_seg, seg_rows, pos0, keep_v, n_cast,
                  attn_heads):
    attn_refs, rest = (rest[:5], rest[5:]) if attn_heads else ((), rest)
    cast_in, rest = rest[:n_cast], rest[n_cast:]
    o_ref, hist_out_ref = rest[:2]
    v_ref = rest[2] if keep_v else None
    cast_out = rest[len(rest) - 1 - n_cast:-1]
    hist_sc = rest[-1]
    _run_side_cast(cast_in, cast_out)
    j = pl.program_id(1)
    d_pool = pscale_ref.shape[-1]
    d_gate = lng_ref.shape[-1]
    pg = d_pool // len(POOL_WINDOWS)
    n_heads = wbig_ref.shape[0]
    gh = d_gate // n_heads

    @pl.when(j == 0)
    def _():
        hist_sc[...] = hist_ref[...]

    x = x_ref[...]
    h = _rmsnorm(x, g_ref[...]).astype(BF16)
    z = _dot(h, win_ref[...])
    p = z[:, :d_pool]
    u = jax.nn.gelu(z[:, d_pool:d_pool + d_gate])
    vpre = jax.nn.gelu(z[:, d_pool + d_gate:])
    mu = jnp.mean(vpre, axis=-1, keepdims=True)
    vc = vpre - mu
    var = jnp.mean(vc * vc, axis=-1, keepdims=True)
    v = vc * lax.rsqrt(var + EPS) * lng_ref[...] + lnb_ref[...]
    if keep_v:
        v_ref[...] = v

    pos = pos0 + j * seg_rows + lax.broadcasted_iota(jnp.int32, (seg_rows, pg), 0)
    pd_segs = []
    for s in range(n_seg):
        ps = p[s * seg_rows:(s + 1) * seg_rows]
        ext = jnp.concatenate([hist_sc[s], ps], axis=0)
        hist_sc[s] = ps[seg_rows - HIST_ROWS:]
        cols = []
        for gi, w in enumerate(POOL_WINDOWS):
            acc = ext[:, gi * pg:(gi + 1) * pg]
            span = 1
            while span < w:
                acc = acc + pltpu.roll(acc, span, 0)
                span *= 2
            cnt = jnp.minimum(pos + 1, w).astype(F32)
            cols.append(acc[HIST_ROWS:] / cnt - ps[:, gi * pg:(gi + 1) * pg])
        pd_segs.append(jnp.concatenate(cols, axis=1))
    pd = pd_segs[0] if n_seg == 1 else jnp.concatenate(pd_segs, axis=0)
    hist_out_ref[...] = hist_sc[...]

    pdb = pd.astype(BF16)
    parts = []
    for gi in range(len(POOL_WINDOWS)):
        sl = slice(gi * pg, (gi + 1) * pg)
        parts.append(_dot(pdb[:, sl], poolw_ref[gi]) * pscale_ref[:, sl])
    vb = v.astype(BF16)
    gate_rows = wbig_ref.shape[1]
    for hi in range(n_heads):
        sl = slice(hi * gh, (hi + 1) * gh)
        mixed = [_dot(wbig_ref[hi], vb[r0:r0 + gate_rows, sl]) + bias_ref[:, sl]
                 for r0 in range(0, x.shape[0], gate_rows)]
        mixed = mixed[0] if len(mixed) == 1 else jnp.concatenate(mixed, axis=0)
        parts.append(u[:, sl] * mixed)
    cat = jnp.concatenate(parts, axis=1).astype(BF16)
    x = x + _dot(cat, wout_ref[...])
    if attn_heads:
        ga_ref, wq_ref, wo_ref, k_ref, v_ref2 = attn_refs
        hd = x.shape[-1] // attn_heads
        q = _queries(x, ga_ref, wq_ref, hd)
        kb = k_ref[...].astype(BF16)
        vb2 = v_ref2[...].astype(BF16)
        o = _attend(q, lambda hi: kb[:, hi * hd:(hi + 1) * hd],
                    lambda hi: vb2[:, hi * hd:(hi + 1) * hd], attn_heads, hd)
        x = x + _dot(o.astype(BF16), wo_ref[...])
    o_ref[...] = x


def _mixer(x, in_place, n_b, n_j, n_seg, seg_rows, pos0, hist, layer, w, wbig, bias, keep_v,
           cast=(), attn=None):
    d = x.shape[1]
    rows = n_seg * seg_rows
    d_in = w["w_in"].shape[-1]
    d_pool = w["pool_scale"].shape[-1]
    d_gate = w["gm_ln_g"].shape[-1]
    vec = lambda a: a.reshape(a.shape[0], 1, a.shape[1])
    xspec = pl.BlockSpec((rows, d), lambda b, j: (b * n_j + j, 0))
    hspec = pl.BlockSpec((n_seg, HIST_ROWS, d_pool), lambda b, j: (b, 0, 0))
    in_specs = [xspec, _const_spec((1, d), layer), _const_spec((d, d_in), layer),
                _const_spec(w["pool_w"].shape[1:], layer), _const_spec((1, d_pool), layer),
                _const_spec((1, d_gate), layer), _const_spec((1, d_gate), layer),
                _const_spec(wbig.shape[1:], layer), _const_spec(bias.shape[1:], layer),
                _const_spec((d, d), layer), hspec]
    out_specs = [xspec, hspec]
    out_shape = [jax.ShapeDtypeStruct(x.shape, F32), jax.ShapeDtypeStruct(hist.shape, F32)]
    if keep_v:
        out_specs.append(pl.BlockSpec((rows, d_gate), lambda b, j: (b * n_j + j, 0)))
        out_shape.append(jax.ShapeDtypeStruct((x.shape[0], d_gate), F32))
    c_in, c_out, c_shapes = _side_cast(cast, n_b * n_j, lambda b, j: b * n_j + j)
    attn_args, attn_heads = [], 0
    if attn is not None:
        mem_k, mem_v, attn_heads = attn
        kvspec = pl.BlockSpec((None, None, mem_k.shape[2], d), lambda b, j: (layer, b, 0, 0))
        in_specs += [_const_spec((1, d), layer), _const_spec((d, d), layer),
                     _const_spec((d, d), layer), kvspec, kvspec]
        attn_args = [vec(w["norm_xa_g"]), w["xa_wq"], w["xa_wo"], mem_k, mem_v]
    return pl.pallas_call(
        functools.partial(_mixer_kernel, n_seg=n_seg, seg_rows=seg_rows, pos0=pos0, keep_v=keep_v,
                          n_cast=len(cast), attn_heads=attn_heads),
        grid=(n_b, n_j),
        in_specs=in_specs + c_in,
        out_specs=out_specs + c_out,
        out_shape=out_shape + c_shapes,
        scratch_shapes=[pltpu.VMEM((n_seg, HIST_ROWS, d_pool), F32)],
        input_output_aliases={0: 0} if in_place else {},
        compiler_params=_params(("arbitrary", "arbitrary")),
        name="mixer_attention" if attn_heads else "token_mixer",
    )(x, vec(w["norm_mix_g"]), w["w_in"], w["pool_w"], vec(w["pool_scale"]),
      vec(w["gm_ln_g"]), vec(w["gm_ln_b"]), wbig, bias, w["w_mix_out"], hist, *attn_args, *cast)


def _gate_operands(ws, bs, rows, chunk):
    depth, n_heads = ws.shape[:2]
    gh = GMLP_CHUNK
    tri = jnp.tril(jnp.ones((chunk, chunk), dtype=bool))
    wc = jnp.where(tri, ws[:, :, :chunk, :chunk], 0)
    eye = jnp.eye(rows // chunk, dtype=ws.dtype)
    wbig = jnp.einsum("ab,lhts->lhatbs", eye, wc).reshape(depth, n_heads, rows, rows).astype(BF16)
    bias = jnp.tile(jnp.swapaxes(bs[:, :, :chunk], 1, 2), (1, rows // chunk, 1))
    bias = jnp.repeat(bias, gh, axis=2)
    return wbig, bias


def _queries(x, g_ref, wq_ref, hd):
    h = _rmsnorm(x, g_ref[...]).astype(BF16)
    return (_dot(h, wq_ref[...]) * (hd ** -0.5)).astype(BF16)


def _attend(qs, k_head, v_head, n_heads, hd):
    outs = []
    for hi in range(n_heads):
        s = lax.dot_general(qs[:, hi * hd:(hi + 1) * hd], k_head(hi), (((1,), (1,)), ((), ())),
                            preferred_element_type=F32)
        e = jnp.exp(s - jnp.max(s, axis=-1, keepdims=True))
        prob = e * (1.0 / jnp.sum(e, axis=-1, keepdims=True))
        outs.append(_dot(prob.astype(BF16), v_head(hi)))
    return jnp.concatenate(outs, axis=1)


def _attn_cached_kernel(x_ref, g_ref, wq_ref, wo_ref, k_hbm, v_hbm, o_ref, kbuf, vbuf, sem, *,
                        layer, streams):
    i = pl.program_id(0)
    n_heads, hd = k_hbm.shape[-2:]
    slot = i % 2

    def copies(step, to_slot):
        src = pl.ds(step * streams, streams)
        return [pltpu.make_async_copy(hbm.at[layer, src, :, hi, :], buf.at[to_slot, hi],
                                      sem.at[to_slot])
                for hbm, buf in ((k_hbm, kbuf), (v_hbm, vbuf)) for hi in range(n_heads)]

    @pl.when(i == 0)
    def _():
        for c in copies(0, 0):
            c.start()

    @pl.when(i + 1 < pl.num_programs(0))
    def _():
        for c in copies(i + 1, 1 - slot):
            c.start()

    x = x_ref[...]
    q = _queries(x, g_ref, wq_ref, hd)
    for c in copies(i, slot):
        c.wait()
    seg = x.shape[0] // streams
    outs = []
    for hi in range(n_heads):
        qh = q[:, hi * hd:(hi + 1) * hd].reshape(streams, seg, hd)
        s = jnp.einsum("sld,smd->slm", qh, kbuf[slot, hi].astype(BF16),
                       preferred_element_type=F32)
        e = jnp.exp(s - jnp.max(s, axis=-1, keepdims=True))
        prob = e * (1.0 / jnp.sum(e, axis=-1, keepdims=True))
        oh = jnp.einsum("slm,smd->sld", prob.astype(BF16), vbuf[slot, hi].astype(BF16),
                        preferred_element_type=F32)
        outs.append(oh.reshape(streams * seg, hd))
    o = jnp.concatenate(outs, axis=1).astype(BF16)
    o_ref[...] = x + _dot(o, wo_ref[...])


def _attention_cached(x, streams, cache_k, cache_v, layer, w):
    d = x.shape[1]
    n_streams, n_mem, n_heads, hd = cache_k.shape[1:]
    rows = x.shape[0] // n_streams * streams
    xspec = pl.BlockSpec((rows, d), lambda i: (i, 0))
    hbm = pl.BlockSpec(memory_space=pl.ANY)
    buf = pltpu.VMEM((2, n_heads, streams, n_mem, hd), F32)
    return pl.pallas_call(
        functools.partial(_attn_cached_kernel, layer=layer, streams=streams),
        grid=(n_streams // streams,),
        in_specs=[xspec, _const_spec((1, d), layer), _const_spec((d, d), layer),
                  _const_spec((d, d), layer), hbm, hbm],
        out_specs=xspec,
        out_shape=jax.ShapeDtypeStruct(x.shape, F32),
        scratch_shapes=[buf, buf, pltpu.SemaphoreType.DMA((2,))],
        input_output_aliases={0: 0},
        compiler_params=_params(("arbitrary",)),
        name="cached_attention",
    )(x, w["norm_xa_g"].reshape(-1, 1, d), w["xa_wq"], w["xa_wo"], cache_k, cache_v)


def _swiglu_into(o_ref, hb, wg_ref, wu_ref, wd_ref, base):
    d_ff = wg_ref.shape[-1]
    for c in range(d_ff // FF_CHUNK):
        sl = slice(c * FF_CHUNK, (c + 1) * FF_CHUNK)
        a = (jax.nn.silu(_dot(hb, wg_ref[:, sl])) * _dot(hb, wu_ref[:, sl])).astype(BF16)
        part = _dot(a, wd_ref[sl, :])
        if c == 0:
            o_ref[...] = part if base is None else base + part
        else:
            o_ref[...] += part


def _ffn_kernel(x_ref, g_ref, wg_ref, wu_ref, wd_ref, gf_ref, *rest, final, n_cast):
    cast_in, o_ref, cast_out = rest[:n_cast], rest[n_cast], rest[n_cast + 1:]
    _run_side_cast(cast_in, cast_out)
    x = x_ref[...]
    hb = _rmsnorm(x, g_ref[...]).astype(BF16)
    _swiglu_into(o_ref, hb, wg_ref, wu_ref, wd_ref, x)
    if final:
        o_ref[...] = _rmsnorm(o_ref[...], gf_ref[...])


def _dense_ffn(x, g, wg, wu, wd, g_final, final, cast=()):
    t, d = x.shape
    d_ff = wg.shape[-1]
    assert t % FFN_ROWS == 0 and d_ff % FF_CHUNK == 0
    n = t // FFN_ROWS
    xspec = pl.BlockSpec((FFN_ROWS, d), lambda i: (i, 0))
    c_in, c_out, c_shapes = _side_cast(cast, n, lambda i: i)
    wspec = lambda shape: _const_spec(shape, single_buffer=True)
    return pl.pallas_call(
        functools.partial(_ffn_kernel, final=final, n_cast=len(cast)),
        grid=(n,),
        in_specs=[xspec, _const_spec((1, d)), wspec((d, d_ff)), wspec((d, d_ff)),
                  wspec((d_ff, d)), _const_spec((1, d))] + c_in,
        out_specs=[xspec] + c_out,
        out_shape=[jax.ShapeDtypeStruct((t, d), F32)] + c_shapes,
        input_output_aliases={0: 0},
        compiler_params=_params(("arbitrary",)),
        name="dense_ffn",
    )(x, g.reshape(1, d), wg, wu, wd, g_final.reshape(1, d), *cast)


def _router_kernel(x_ref, g_ref, rw_ref, rb_ref, before_ref, cnt0_ref, sel_ref, gate_ref, cnt_ref,
                   carry):
    i = pl.program_id(0)

    @pl.when(i == 0)
    def _():
        carry[...] = cnt0_ref[...]

    h = _rmsnorm(x_ref[...], g_ref[...])
    h_hi = h.astype(BF16)
    h_lo = (h - h_hi.astype(F32)).astype(BF16)
    by_hi = _dot(h_hi, rw_ref[...])
    logits = by_hi[:, :LANES] + (by_hi[:, LANES:] + _dot(h_lo, rw_ref[:, :LANES])) + rb_ref[...]
    lane_i = lax.broadcasted_iota(jnp.int32, logits.shape, 1)
    lane = lane_i.astype(F32)
    m0 = jnp.max(logits, axis=-1, keepdims=True)
    e0 = jnp.min(jnp.where(logits == m0, lane, float(LANES)), axis=-1, keepdims=True)
    rest = jnp.where(lane == e0, -jnp.inf, logits)
    m1 = jnp.max(rest, axis=-1, keepdims=True)
    e1 = jnp.min(jnp.where(rest == m1, lane, float(LANES)), axis=-1, keepdims=True)
    t = jnp.exp(m1 - m0)
    g0 = 1.0 / (1.0 + t)
    g1 = t * g0
    hot0 = (lane == e0).astype(F32)
    hot1 = (lane == e1).astype(F32)
    both = hot0 + hot1
    prior = _dot(before_ref[...], both.astype(BF16)) + carry[...]
    rank0 = jnp.sum(hot0 * prior, axis=-1, keepdims=True)
    rank1 = jnp.sum(hot1 * prior, axis=-1, keepdims=True)
    carry[...] += jnp.sum(both, axis=0, keepdims=True)
    cnt_ref[...] = carry[...]
    sel = jnp.where(lane_i == 0, e0, jnp.where(lane_i == 1, e1,
                    jnp.where(lane_i == 2, rank0, jnp.where(lane_i == 3, rank1, 0.0))))
    sel_ref[...] = jnp.transpose(sel)[:SUBLANES]
    gate_ref[...] = jnp.where(lane_i == 0, g0, jnp.where(lane_i == 1, g1, 0.0))


def _router(x, g, rw_parts, rb_pad, counts_before):
    t, d = x.shape
    rows = FFN_ROWS
    before = jnp.tril(jnp.ones((rows, rows), F32), -1).astype(BF16)
    return pl.pallas_call(
        _router_kernel,
        grid=(t // rows,),
        in_specs=[pl.BlockSpec((rows, d), lambda i: (i, 0)), _const_spec((1, d)),
                  _const_spec((d, 2 * LANES)), _const_spec((1, LANES)), _const_spec((rows, rows)),
                  _const_spec((1, LANES))],
        out_specs=[pl.BlockSpec((SUBLANES, rows), lambda i: (0, i)),
                   pl.BlockSpec((rows, LANES), lambda i: (i, 0)), _const_spec((1, LANES))],
        out_shape=[jax.ShapeDtypeStruct((SUBLANES, t), F32),
                   jax.ShapeDtypeStruct((t, LANES), F32),
                   jax.ShapeDtypeStruct((1, LANES), F32)],
        scratch_shapes=[pltpu.VMEM((1, LANES), F32)],
        compiler_params=_params(("arbitrary",)),
        name="moe_router",
    )(x, g.reshape(1, d), rw_parts, rb_pad, before, counts_before)


def _dispatch_kernel(meta_ref, xa_ref, xb_ref, p0_ref, p1_ref, xs_ref, zero_sc, sem, zsem, *,
                     n_a, n_exp, min_tiles):
    i = pl.program_id(0)
    rows = xa_ref.shape[0] * SUBLANES
    tile = zero_sc.shape[0]
    n_tiles = xs_ref.shape[0] // tile

    def zero_copy(row_end):
        start = pl.multiple_of(row_end - tile, tile)
        return pltpu.make_async_copy(zero_sc, xs_ref.at[pl.ds(start, tile)], zsem)

    fills = [(meta_ref[n_exp + e] > 0, meta_ref[e]) for e in range(n_exp)]
    fills += [(k >= meta_ref[2 * n_exp], (k + 1) * tile) for k in range(min_tiles, n_tiles)]

    @pl.when(i == 0)
    def _():
        zero_sc[...] = jnp.zeros_like(zero_sc)
        for needed, row_end in fills:
            @pl.when(needed)
            def _():
                zero_copy(row_end).start()
        for needed, row_end in fills:
            @pl.when(needed)
            def _():
                zero_copy(row_end).wait()

    def scatter_rows(x_ref):
        def start(grp, c):
            base = pl.multiple_of(grp * SUBLANES, SUBLANES)
            for u in range(SUBLANES):
                src = x_ref.at[grp, pl.ds(u, 1)]
                pltpu.make_async_copy(src, xs_ref.at[pl.ds(p0_ref[base + u], 1)], sem).start(0)
                pltpu.make_async_copy(src, xs_ref.at[pl.ds(p1_ref[base + u], 1)], sem).start(1)
            return c

        lax.fori_loop(0, rows // SUBLANES, start, 0)
        for _ in range(TOP_K * rows // tile):
            pltpu.make_async_copy(zero_sc, xs_ref.at[pl.ds(0, tile)], sem).wait()

    @pl.when(i < n_a)
    def _():
        scatter_rows(xa_ref)

    @pl.when(i >= n_a)
    def _():
        scatter_rows(xb_ref)


def _dispatch(xa, xb, pos0, pos1, meta, n_exp, n_sorted, tile):
    d = xa.shape[1]
    rows = DISPATCH_ROWS
    n_a, n_b = xa.shape[0] // rows, xb.shape[0] // rows
    t = xa.shape[0] + xb.shape[0]
    smem_rows = pl.BlockSpec((rows,), lambda i, *_: (i,), memory_space=pltpu.SMEM)
    min_tiles = -(-TOP_K * t // tile)
    assert rows % tile == 0
    grp = rows // SUBLANES
    grouped = lambda x: x.reshape(x.shape[0] // SUBLANES, SUBLANES, d)
    x_specs = [pl.BlockSpec((grp, SUBLANES, d), lambda i, *_: (jnp.minimum(i, n_a - 1), 0, 0)),
               pl.BlockSpec((grp, SUBLANES, d), lambda i, *_: (jnp.maximum(i - n_a, 0), 0, 0))]
    return pl.pallas_call(
        functools.partial(_dispatch_kernel, n_a=n_a, n_exp=n_exp, min_tiles=min_tiles),
        grid_spec=pltpu.PrefetchScalarGridSpec(
            num_scalar_prefetch=1,
            grid=(n_a + n_b,),
            in_specs=x_specs + [smem_rows, smem_rows],
            out_specs=pl.BlockSpec(memory_space=pl.ANY),
            scratch_shapes=[pltpu.VMEM((tile, d), F32), pltpu.SemaphoreType.DMA,
                            pltpu.SemaphoreType.DMA]),
        out_shape=jax.ShapeDtypeStruct((n_sorted, d), F32),
        compiler_params=_params(("arbitrary",)),
        name="moe_dispatch",
    )(meta, grouped(xa), grouped(xb), pos0, pos1)


def _expert_kernel(te_ref, nu_ref, xs_ref, g_ref, wg_ref, wu_ref, wd_ref, ys_ref):
    used = pl.program_id(0) < nu_ref[0]

    @pl.when(used)
    def _():
        hb = _rmsnorm(xs_ref[...], g_ref[...]).astype(BF16)
        _swiglu_into(ys_ref, hb, wg_ref, wu_ref, wd_ref, None)

    @pl.when(jnp.logical_not(used))
    def _():
        ys_ref[...] = jnp.zeros_like(ys_ref)


def _expert_ffn(xs, g, tile_expert, n_used, wg, wu, wd, tile):
    n_sorted, d = xs.shape
    d_ff = wg.shape[-1]
    rowmap = lambda i, te, nu: (jnp.minimum(i, nu[0] - 1), 0)
    wmap = lambda i, te, nu: (te[i], 0, 0)
    return pl.pallas_call(
        _expert_kernel,
        grid_spec=pltpu.PrefetchScalarGridSpec(
            num_scalar_prefetch=2,
            grid=(n_sorted // tile,),
            in_specs=[pl.BlockSpec((tile, d), rowmap),
                      pl.BlockSpec((1, d), lambda i, te, nu: (0, 0)),
                      pl.BlockSpec((None, d, d_ff), wmap),
                      pl.BlockSpec((None, d, d_ff), wmap),
                      pl.BlockSpec((None, d_ff, d), wmap)],
            out_specs=pl.BlockSpec((tile, d), lambda i, te, nu: (i, 0))),
        out_shape=jax.ShapeDtypeStruct((n_sorted, d), F32),
        compiler_params=_params(("arbitrary",)),
        name="expert_ffn",
    )(tile_expert, n_used, xs, g.reshape(1, d), wg, wu, wd)


def _combine_kernel(x_ref, p0_ref, p1_ref, p0n_ref, p1n_ref, gate_ref, gf_ref, ys_ref, o_ref, buf,
                    sem, *, final):
    i = pl.program_id(0)
    rows, d = x_ref.shape
    slot = i % 2

    def gather(pa_ref, pb_ref, to_slot):
        def start(grp, c):
            base = pl.multiple_of(grp * SUBLANES, SUBLANES)
            for u in range(SUBLANES):
                for k, p_ref in enumerate((pa_ref, pb_ref)):
                    pltpu.make_async_copy(ys_ref.at[pl.ds(p_ref[base + u], 1)],
                                          buf.at[to_slot, k, grp, pl.ds(u, 1)],
                                          sem.at[to_slot]).start(k)
            return c

        lax.fori_loop(0, rows // SUBLANES, start, 0)

    @pl.when(i == 0)
    def _():
        gather(p0_ref, p1_ref, 0)

    @pl.when(i + 1 < pl.num_programs(0))
    def _():
        gather(p0n_ref, p1n_ref, 1 - slot)

    for k in range(TOP_K):
        pltpu.make_async_copy(ys_ref.at[pl.ds(0, rows)], o_ref, sem.at[slot]).wait()
    gates = gate_ref[...]
    y0, y1 = (buf[slot, k].reshape(rows, d) for k in range(TOP_K))
    out = x_ref[...] + gates[:, 0:1] * y0 + gates[:, 1:2] * y1
    if final:
        out = _rmsnorm(out, gf_ref[...])
    o_ref[...] = out


def _combine(x, off_rows, pos0, pos1, gates, ys, g_final, final):
    n_rows, d = x.shape
    rows = COMBINE_ROWS
    off = off_rows // rows
    n = n_rows // rows
    smem_rows = pl.BlockSpec((rows,), lambda i: (off + i,), memory_space=pltpu.SMEM)
    smem_next = pl.BlockSpec((rows,), lambda i: (off + jnp.minimum(i + 1, n - 1),),
                             memory_space=pltpu.SMEM)
    xspec = pl.BlockSpec((rows, d), lambda i: (i, 0))
    return pl.pallas_call(
        functools.partial(_combine_kernel, final=final),
        grid=(n,),
        in_specs=[xspec, smem_rows, smem_rows, smem_next, smem_next,
                  pl.BlockSpec((rows, LANES), lambda i: (i, 0)), _const_spec((1, d)),
                  pl.BlockSpec(memory_space=pl.ANY)],
        out_specs=xspec,
        out_shape=jax.ShapeDtypeStruct((n_rows, d), F32),
        scratch_shapes=[pltpu.VMEM((2, TOP_K, rows // SUBLANES, SUBLANES, d), F32),
                        pltpu.SemaphoreType.DMA((2,))],
        compiler_params=_params(("arbitrary",)),
        name="moe_combine",
    )(x, pos0, pos1, pos0, pos1, gates, g_final.reshape(1, d), ys)


def _moe_layer(xa, xb, g, rw, rb, wg, wu, wd, g_final, final):
    d = xa.shape[1]
    t = xa.shape[0] + xb.shape[0]
    n_exp = rw.shape[-1]
    tile = FFN_ROWS
    rw_pad = jnp.zeros((d, LANES), F32).at[:, :n_exp].set(rw)
    rw_hi = rw_pad.astype(BF16)
    rw_parts = jnp.concatenate([rw_hi, (rw_pad - rw_hi.astype(F32)).astype(BF16)], axis=1)
    rb_pad = jnp.full((1, LANES), -jnp.inf, F32).at[0, :n_exp].set(rb)
    sel_a, gates_a, counts_a = _router(xa, g, rw_parts, rb_pad, jnp.zeros((1, LANES), F32))
    sel_b, gates_b, counts = _router(xb, g, rw_parts, rb_pad, counts_a)
    sel = jnp.concatenate([sel_a, sel_b], axis=1)

    counts = counts[0, :n_exp].astype(jnp.int32)
    gpad = (counts + tile - 1) // tile * tile
    gend = jnp.cumsum(gpad)
    gstart = gend - gpad
    sel = sel.astype(jnp.int32)
    pos0 = gstart[sel[0]] + sel[2]
    pos1 = gstart[sel[1]] + sel[3]
    n_tiles = (TOP_K * t + n_exp * (tile - 1)) // tile
    tile_ids = jnp.arange(n_tiles, dtype=jnp.int32)
    tile_expert = jnp.minimum(
        jnp.sum((gend // tile)[None, :] <= tile_ids[:, None], axis=1), n_exp - 1).astype(jnp.int32)
    n_used = (gend[-1:] // tile).astype(jnp.int32)

    meta = jnp.concatenate([gend, gpad, n_used]).astype(jnp.int32)
    xs = _dispatch(xa, xb, pos0, pos1, meta, n_exp, n_tiles * tile, tile)
    ys = _expert_ffn(xs, g, tile_expert, n_used, wg, wu, wd, tile)
    return (_combine(xa, 0, pos0, pos1, gates_a, ys, g_final, final),
            _combine(xb, xa.shape[0], pos0, pos1, gates_b, ys, g_final, final))


def kernel(x_prompt, x_sample, cache_mem_k, cache_mem_v, state_pool, mem_prompt, norm_mix_g, w_in, pool_w, pool_scale, gm_ln_g, gm_ln_b, gm_ws, gm_bs, w_mix_out, norm_xa_g, norm_mem_g, xa_wq, xa_wk, xa_wv, xa_wo, norm_ffn_g, ffn_wg, ffn_wu, ffn_wd, moe_router_w, moe_router_b, moe_wg, moe_wu, moe_wd, norm_final_g):
    batch, seq, d = x_prompt.shape
    dec_batch, dec_seq, _ = x_sample.shape
    depth = norm_mix_g.shape[0]
    n_mem, n_mem_heads, mem_hd = cache_mem_k.shape[2:]
    d_pool = pool_scale.shape[-1]
    d_gate = gm_ln_g.shape[-1]
    rows_p, rows_s = batch * seq, dec_batch * dec_seq
    seg_per_tile = GATE_ROWS // dec_seq
    assert seq % MIX_ROWS == 0 and MIX_ROWS % GATE_ROWS == 0
    assert GATE_ROWS % GMLP_CHUNK == 0
    assert GATE_ROWS % dec_seq == 0 and dec_batch % seg_per_tile == 0 and dec_seq >= HIST_ROWS
    assert dec_seq <= GMLP_CHUNK and PAST_LEN % GMLP_CHUNK == 0
    assert dec_batch % ATTN_SAMPLE_STREAMS == 0
    assert rows_p % FFN_ROWS == 0 and rows_s % FFN_ROWS == 0
    assert rows_p % DISPATCH_ROWS == 0 and rows_s % DISPATCH_ROWS == 0
    assert rows_p % COMBINE_ROWS == 0 and rows_s % COMBINE_ROWS == 0

    w = dict(norm_mix_g=norm_mix_g, w_in=w_in.astype(BF16), pool_w=pool_w.astype(BF16),
             pool_scale=pool_scale, gm_ln_g=gm_ln_g, gm_ln_b=gm_ln_b,
             w_mix_out=w_mix_out.astype(BF16), norm_xa_g=norm_xa_g,
             xa_wq=xa_wq.astype(BF16), xa_wo=xa_wo.astype(BF16))

    mk, mv, new_mem_k, new_mem_v = _memory_kv(mem_prompt, norm_mem_g, xa_wk, xa_wv, n_mem_heads)
    pk, pv = mk.reshape(depth, batch, n_mem, d), mv.reshape(depth, batch, n_mem, d)

    hist_p0 = jnp.zeros((batch, HIST_ROWS, d_pool), F32)
    hist_s = jnp.pad(state_pool, ((0, 0), (0, 0), (HIST_ROWS - state_pool.shape[2], 0), (0, 0)))
    keep = HIST_ROWS - state_pool.shape[2]

    xp = x_prompt.reshape(rows_p, d)
    xs = x_sample.reshape(rows_s, d)
    pool_p, pool_s, v_rows = [], [], []
    n_exp, _, d_ff = moe_wg.shape[1:]
    expert_w = {}
    wbig_p, bias_p = _gate_operands(gm_ws, gm_bs, GATE_ROWS, GMLP_CHUNK)
    wbig_s, bias_s = _gate_operands(gm_ws, gm_bs, GATE_ROWS, dec_seq)
    for l in range(depth):
        if l % 2 == 1:
            mix_cast = [moe_wd[l // 2].reshape(n_exp * d_ff, d)]
        else:
            mix_cast = [ffn_wg[l // 2], ffn_wu[l // 2], ffn_wd[l // 2]]
        xp, hp, *mix_done = _mixer(xp, l > 0, batch, seq // MIX_ROWS, 1, MIX_ROWS, 0, hist_p0, l,
                                   w, wbig_p, bias_p, False, mix_cast, (pk, pv, n_mem_heads))
        if l % 2 == 1:
            expert_w["wd"] = mix_done[0].reshape(n_exp, d_ff, d)
        else:
            ffn_b = mix_done
        xs, hs, vr = _mixer(xs, l > 0, dec_batch // seg_per_tile, 1, seg_per_tile, dec_seq,
                            PAST_LEN, hist_s[l], l, w, wbig_s, bias_s, True)
        pool_p.append(hp[:, keep:])
        pool_s.append(hs[:, keep:])
        v_rows.append(vr.reshape(dec_batch, dec_seq, d_gate))

        xs = _attention_cached(xs, ATTN_SAMPLE_STREAMS, cache_mem_k, cache_mem_v, l, w)

        final = l == depth - 1
        if l % 2 == 0:
            ffn_cast = ([moe_wg[l // 2].reshape(n_exp * d, d_ff),
                         moe_wu[l // 2].reshape(n_exp * d, d_ff)] if l + 1 < depth else [])
            xp, *ffn_done = _dense_ffn(xp, norm_ffn_g[l], *ffn_b, norm_final_g, final, ffn_cast)
            xs, = _dense_ffn(xs, norm_ffn_g[l], *ffn_b, norm_final_g, final)
            if ffn_done:
                expert_w["wg"], expert_w["wu"] = (a.reshape(n_exp, d, d_ff) for a in ffn_done)
        else:
            i = l // 2
            xp, xs = _moe_layer(xp, xs, norm_ffn_g[l], moe_router_w[i], moe_router_b[i],
                                expert_w["wg"], expert_w["wu"], expert_w["wd"], norm_final_g, final)

    return (xp.reshape(batch, seq, d), xs.reshape(dec_batch, dec_seq, d),
            new_mem_k, new_mem_v, jnp.stack(pool_p), jnp.stack(pool_s), jnp.stack(v_rows))
```

```python
import functools

import jax
import jax.numpy as jnp
from jax import lax
from jax.experimental import pallas as pl
from jax.experimental.pallas import tpu as pltpu

EPS = 1e-6
PAST_LEN = 4096
POOL_WINDOWS = (2, 4, 8, 16)
HIST_ROWS = 16
GMLP_CHUNK = 128
LANES = 128
SUBLANES = 8
TOP_K = 2

V7X_VMEM_BYTES = 64 * 1024 * 1024
VMEM_LIMIT = V7X_VMEM_BYTES - 8 * 1024 * 1024

MIX_ROWS = 1024
MIX_SUB = 2
GATE_ROWS = 256
ATTN_SAMPLE_STREAMS = 8
FFN_ROWS = 512
FF_CHUNK = 256
DISPATCH_ROWS = 1024
COMBINE_ROWS = 512

BF16 = jnp.bfloat16
F32 = jnp.float32


def _rmsnorm(x, g):
    return x * lax.rsqrt(jnp.mean(x * x, axis=-1, keepdims=True) + EPS) * g


def _dot(a, b):
    return jnp.dot(a, b, preferred_element_type=F32)


def _params(sem, vmem=VMEM_LIMIT):
    return pltpu.CompilerParams(dimension_semantics=sem, vmem_limit_bytes=vmem)


def _const_spec(shape, layer=None, single_buffer=False):
    nd = len(shape)
    mode = dict(pipeline_mode=pl.Buffered(1)) if single_buffer else {}
    if layer is None:
        return pl.BlockSpec(shape, lambda *_: (0,) * nd, **mode)
    return pl.BlockSpec((None,) + shape, lambda *_: (layer,) + (0,) * nd, **mode)


def _side_cast(arrays, n_steps, step_of):
    specs, shapes = [], []
    for a in arrays:
        rows, cols = a.shape
        n_blk = n_steps
        while rows % (n_blk * 2 * SUBLANES):
            n_blk -= 1
        imap = lambda *ids, n_blk=n_blk: (jnp.minimum(step_of(*ids), n_blk - 1), 0)
        specs.append(pl.BlockSpec((rows // n_blk, cols), imap))
        shapes.append(jax.ShapeDtypeStruct(a.shape, BF16))
    return specs, list(specs), shapes


def _run_side_cast(in_refs, out_refs):
    for src, dst in zip(in_refs, out_refs):
        dst[...] = src[...].astype(BF16)


def _memkv_kernel(mem_ref, g_ref, wk_ref, wv_ref, k_ref, v_ref, k5_ref, v5_ref, sem):
    l = pl.program_id(0)
    m = _rmsnorm(mem_ref[...], g_ref[...]).astype(BF16)
    k_ref[0] = _dot(m, wk_ref[...].astype(BF16))
    v_ref[0] = _dot(m, wv_ref[...].astype(BF16))
    depth, n_b, n_mem, n_heads, hd = k5_ref.shape
    for layer in range(depth):
        @pl.when(l == layer)
        def _():
            copies = [pltpu.make_async_copy(src.at[0, pl.ds(b * n_mem, n_mem), pl.ds(h * hd, hd)],
                                            dst.at[layer, b, :, h, :], sem)
                      for src, dst in ((k_ref, k5_ref), (v_ref, v5_ref))
                      for b in range(n_b) for h in range(n_heads)]
            for c in copies:
                c.start()
            for c in copies:
                c.wait()


def _memory_kv(mem, norm_g, wk, wv, n_heads):
    depth, d = norm_g.shape
    n_b, n_mem, _ = mem.shape
    rows = n_b * n_mem
    lay = lambda l: (l, 0, 0)
    five_d = jax.ShapeDtypeStruct((depth, n_b, n_mem, n_heads, d // n_heads), F32)
    return pl.pallas_call(
        _memkv_kernel,
        grid=(depth,),
        in_specs=[pl.BlockSpec((rows, d), lambda l: (0, 0)),
                  pl.BlockSpec((None, 1, d), lay),
                  pl.BlockSpec((None, d, d), lay),
                  pl.BlockSpec((None, d, d), lay)],
        out_specs=[pl.BlockSpec((1, rows, d), lay)] * 2
                  + [pl.BlockSpec(memory_space=pl.ANY)] * 2,
        out_shape=[jax.ShapeDtypeStruct((depth, rows, d), F32)] * 2 + [five_d] * 2,
        scratch_shapes=[pltpu.SemaphoreType.DMA],
        compiler_params=_params(("arbitrary",)),
        name="memory_kv",
    )(mem.reshape(rows, d), norm_g.reshape(depth, 1, d), wk, wv)


def _mixer_kernel(x_ref, g_ref, win_ref, poolw_ref, pscale_ref, lng_ref, lnb_ref, wbig_ref,
                  bias_ref, wout_ref, hist_ref, *rest, n_seg, seg_rows, pos0, keep_v, n_cast,
                  attn_heads, sub):
    attn_refs, rest = (rest[:5], rest[5:]) if attn_heads else ((), rest)
    cast_in, rest = rest[:n_cast], rest[n_cast:]
    o_ref, hist_out_ref = rest[:2]
    v_ref = rest[2] if keep_v else None
    cast_out = rest[len(rest) - 1 - n_cast:-1]
    hist_sc = rest[-1]
    _run_side_cast(cast_in, cast_out)
    j = pl.program_id(1)
    d_pool = pscale_ref.shape[-1]
    d_gate = lng_ref.shape[-1]
    pg = d_pool // len(POOL_WINDOWS)
    n_heads = wbig_ref.shape[0]
    gh = d_gate // n_heads

    @pl.when(j == 0)
    def _():
        hist_sc[...] = hist_ref[...]

    def mix(x, step):
        h = _rmsnorm(x, g_ref[...]).astype(BF16)
        z = _dot(h, win_ref[...])
        p = z[:, :d_pool]
        u = jax.nn.gelu(z[:, d_pool:d_pool + d_gate])
        vpre = jax.nn.gelu(z[:, d_pool + d_gate:])
        mu = jnp.mean(vpre, axis=-1, keepdims=True)
        vc = vpre - mu
        var = jnp.mean(vc * vc, axis=-1, keepdims=True)
        v = vc * lax.rsqrt(var + EPS) * lng_ref[...] + lnb_ref[...]

        pos = pos0 + step * seg_rows + lax.broadcasted_iota(jnp.int32, (seg_rows, pg), 0)
        pd_segs = []
        for s in range(n_seg):
            ps = p[s * seg_rows:(s + 1) * seg_rows]
            ext = jnp.concatenate([hist_sc[s], ps], axis=0)
            hist_sc[s] = ps[seg_rows - HIST_ROWS:]
            cols = []
            for gi, w in enumerate(POOL_WINDOWS):
                acc = ext[:, gi * pg:(gi + 1) * pg]
                span = 1
                while span < w:
                    acc = acc + pltpu.roll(acc, span, 0)
                    span *= 2
                cnt = jnp.minimum(pos + 1, w).astype(F32)
                cols.append(acc[HIST_ROWS:] / cnt - ps[:, gi * pg:(gi + 1) * pg])
            pd_segs.append(jnp.concatenate(cols, axis=1))
        pd = pd_segs[0] if n_seg == 1 else jnp.concatenate(pd_segs, axis=0)

        pdb = pd.astype(BF16)
        parts = []
        for gi in range(len(POOL_WINDOWS)):
            sl = slice(gi * pg, (gi + 1) * pg)
            parts.append(_dot(pdb[:, sl], poolw_ref[gi]) * pscale_ref[:, sl])
        vb = v.astype(BF16)
        gate_rows = wbig_ref.shape[1]
        for hi in range(n_heads):
            sl = slice(hi * gh, (hi + 1) * gh)
            mixed = [_dot(wbig_ref[hi], vb[r0:r0 + gate_rows, sl]) + bias_ref[:, sl]
                     for r0 in range(0, x.shape[0], gate_rows)]
            mixed = mixed[0] if len(mixed) == 1 else jnp.concatenate(mixed, axis=0)
            parts.append(u[:, sl] * mixed)
        cat = jnp.concatenate(parts, axis=1).astype(BF16)
        return x + _dot(cat, wout_ref[...]), v

    def attend(x):
        ga_ref, wq_ref, wo_ref, k_ref, v_ref2 = attn_refs
        hd = x.shape[-1] // attn_heads
        q = _queries(x, ga_ref, wq_ref, hd)
        kb = k_ref[...].astype(BF16)
        vb2 = v_ref2[...].astype(BF16)
        o = _attend(q, lambda hi: kb[:, hi * hd:(hi + 1) * hd],
                    lambda hi: vb2[:, hi * hd:(hi + 1) * hd], attn_heads, hd)
        return x + _dot(o.astype(BF16), wo_ref[...])

    sub_rows = n_seg * seg_rows
    mixed_x = []
    for s in range(sub):
        rs = slice(s * sub_rows, (s + 1) * sub_rows)
        xm, v = mix(x_ref[rs, :], j * sub + s)
        if keep_v:
            v_ref[rs, :] = v
        mixed_x.append(xm)
    hist_out_ref[...] = hist_sc[...]
    for s in range(sub):
        rs = slice(s * sub_rows, (s + 1) * sub_rows)
        o_ref[rs, :] = attend(mixed_x[s]) if attn_heads else mixed_x[s]


def _mixer(x, in_place, n_b, n_j, n_seg, seg_rows, pos0, hist, layer, w, wbig, bias, keep_v,
           cast=(), attn=None, sub=1):
    d = x.shape[1]
    rows = sub * n_seg * seg_rows
    d_in = w["w_in"].shape[-1]
    d_pool = w["pool_scale"].shape[-1]
    d_gate = w["gm_ln_g"].shape[-1]
    vec = lambda a: a[layer].reshape(1, -1)
    xspec = pl.BlockSpec((rows, d), lambda b, j: (b * n_j + j, 0))
    hspec = pl.BlockSpec((n_seg, HIST_ROWS, d_pool), lambda b, j: (b, 0, 0))
    in_specs = [xspec, _const_spec((1, d)), _const_spec((d, d_in), layer),
                _const_spec(w["pool_w"].shape[1:], layer),
                _const_spec((1, d_pool)), _const_spec((1, d_gate)), _const_spec((1, d_gate)),
                _const_spec(wbig.shape), _const_spec(bias.shape), _const_spec((d, d), layer),
                hspec]
    out_specs = [xspec, hspec]
    out_shape = [jax.ShapeDtypeStruct(x.shape, F32), jax.ShapeDtypeStruct(hist.shape, F32)]
    if keep_v:
        out_specs.append(pl.BlockSpec((rows, d_gate), lambda b, j: (b * n_j + j, 0)))
        out_shape.append(jax.ShapeDtypeStruct((x.shape[0], d_gate), F32))
    c_in, c_out, c_shapes = _side_cast(cast, n_b * n_j, lambda b, j: b * n_j + j)
    attn_args, attn_heads = [], 0
    if attn is not None:
        mem_k, mem_v, attn_heads = attn
        kvspec = pl.BlockSpec((None, None, mem_k.shape[2], d), lambda b, j: (layer, b, 0, 0))
        in_specs += [_const_spec((1, d)), _const_spec((d, d), layer), _const_spec((d, d), layer),
                     kvspec, kvspec]
        attn_args = [vec(w["norm_xa_g"]), w["xa_wq"], w["xa_wo"], mem_k, mem_v]
    return pl.pallas_call(
        functools.partial(_mixer_kernel, n_seg=n_seg, seg_rows=seg_rows, pos0=pos0, keep_v=keep_v,
                          n_cast=len(cast), attn_heads=attn_heads, sub=sub),
        grid=(n_b, n_j),
        in_specs=in_specs + c_in,
        out_specs=out_specs + c_out,
        out_shape=out_shape + c_shapes,
        scratch_shapes=[pltpu.VMEM((n_seg, HIST_ROWS, d_pool), F32)],
        input_output_aliases={0: 0} if in_place else {},
        compiler_params=_params(("arbitrary", "arbitrary")),
        name="mixer_attention" if attn_heads else "token_mixer",
    )(x, vec(w["norm_mix_g"]), w["w_in"], w["pool_w"], vec(w["pool_scale"]),
      vec(w["gm_ln_g"]), vec(w["gm_ln_b"]), wbig, bias, w["w_mix_out"], hist, *attn_args, *cast)


def _gate_operands(ws, bs, rows, chunk):
    n_heads = ws.shape[0]
    gh = GMLP_CHUNK
    tri = jnp.tril(jnp.ones((chunk, chunk), dtype=bool))
    wc = jnp.where(tri[None], ws[:, :chunk, :chunk], 0)
    eye = jnp.eye(rows // chunk, dtype=ws.dtype)
    wbig = jnp.einsum("ab,hts->hatbs", eye, wc).reshape(n_heads, rows, rows).astype(BF16)
    bias = jnp.tile(bs[:, :chunk].T, (rows // chunk, 1))
    bias = jnp.repeat(bias, gh, axis=1)
    return wbig, bias


def _queries(x, g_ref, wq_ref, hd):
    h = _rmsnorm(x, g_ref[...]).astype(BF16)
    return (_dot(h, wq_ref[...]) * (hd ** -0.5)).astype(BF16)


def _attend(qs, k_head, v_head, n_heads, hd):
    outs = []
    for hi in range(n_heads):
        s = lax.dot_general(qs[:, hi * hd:(hi + 1) * hd], k_head(hi), (((1,), (1,)), ((), ())),
                            preferred_element_type=F32)
        e = jnp.exp(s - jnp.max(s, axis=-1, keepdims=True))
        prob = e * (1.0 / jnp.sum(e, axis=-1, keepdims=True))
        outs.append(_dot(prob.astype(BF16), v_head(hi)))
    return jnp.concatenate(outs, axis=1)


def _attn_cached_kernel(x_ref, g_ref, wq_ref, wo_ref, k_hbm, v_hbm, o_ref, kbuf, vbuf, sem, *,
                        layer, streams):
    i = pl.program_id(0)
    n_heads, hd = k_hbm.shape[-2:]
    slot = i % 2

    def copies(step, to_slot):
        src = pl.ds(step * streams, streams)
        return [pltpu.make_async_copy(hbm.at[layer, src, :, hi, :], buf.at[to_slot, hi],
                                      sem.at[to_slot])
                for hbm, buf in ((k_hbm, kbuf), (v_hbm, vbuf)) for hi in range(n_heads)]

    @pl.when(i == 0)
    def _():
        for c in copies(0, 0):
            c.start()

    @pl.when(i + 1 < pl.num_programs(0))
    def _():
        for c in copies(i + 1, 1 - slot):
            c.start()

    x = x_ref[...]
    q = _queries(x, g_ref, wq_ref, hd)
    for c in copies(i, slot):
        c.wait()
    seg = x.shape[0] // streams
    outs = []
    for hi in range(n_heads):
        qh = q[:, hi * hd:(hi + 1) * hd].reshape(streams, seg, hd)
        s = jnp.einsum("sld,smd->slm", qh, kbuf[slot, hi].astype(BF16),
                       preferred_element_type=F32)
        e = jnp.exp(s - jnp.max(s, axis=-1, keepdims=True))
        prob = e * (1.0 / jnp.sum(e, axis=-1, keepdims=True))
        oh = jnp.einsum("slm,smd->sld", prob.astype(BF16), vbuf[slot, hi].astype(BF16),
                        preferred_element_type=F32)
        outs.append(oh.reshape(streams * seg, hd))
    o = jnp.concatenate(outs, axis=1).astype(BF16)
    o_ref[...] = x + _dot(o, wo_ref[...])


def _attention_cached(x, streams, cache_k, cache_v, layer, w):
    d = x.shape[1]
    n_streams, n_mem, n_heads, hd = cache_k.shape[1:]
    rows = x.shape[0] // n_streams * streams
    xspec = pl.BlockSpec((rows, d), lambda i: (i, 0))
    hbm = pl.BlockSpec(memory_space=pl.ANY)
    buf = pltpu.VMEM((2, n_heads, streams, n_mem, hd), F32)
    return pl.pallas_call(
        functools.partial(_attn_cached_kernel, layer=layer, streams=streams),
        grid=(n_streams // streams,),
        in_specs=[xspec, _const_spec((1, d)), _const_spec((d, d), layer),
                  _const_spec((d, d), layer), hbm, hbm],
        out_specs=xspec,
        out_shape=jax.ShapeDtypeStruct(x.shape, F32),
        scratch_shapes=[buf, buf, pltpu.SemaphoreType.DMA((2,))],
        input_output_aliases={0: 0},
        compiler_params=_params(("arbitrary",)),
        name="cached_attention",
    )(x, w["norm_xa_g"][layer].reshape(1, d), w["xa_wq"], w["xa_wo"], cache_k, cache_v)


def _swiglu_into(o_ref, hb, wg_ref, wu_ref, wd_ref, base):
    d_ff = wg_ref.shape[-1]
    for c in range(d_ff // FF_CHUNK):
        sl = slice(c * FF_CHUNK, (c + 1) * FF_CHUNK)
        a = (jax.nn.silu(_dot(hb, wg_ref[:, sl])) * _dot(hb, wu_ref[:, sl])).astype(BF16)
        part = _dot(a, wd_ref[sl, :])
        if c == 0:
            o_ref[...] = part if base is None else base + part
        else:
            o_ref[...] += part


def _ffn_kernel(x_ref, g_ref, wg_ref, wu_ref, wd_ref, gf_ref, *rest, final, n_cast):
    cast_in, o_ref, cast_out = rest[:n_cast], rest[n_cast], rest[n_cast + 1:]
    _run_side_cast(cast_in, cast_out)
    x = x_ref[...]
    hb = _rmsnorm(x, g_ref[...]).astype(BF16)
    _swiglu_into(o_ref, hb, wg_ref, wu_ref, wd_ref, x)
    if final:
        o_ref[...] = _rmsnorm(o_ref[...], gf_ref[...])


def _dense_ffn(x, g, wg, wu, wd, g_final, final, cast=()):
    t, d = x.shape
    d_ff = wg.shape[-1]
    assert t % FFN_ROWS == 0 and d_ff % FF_CHUNK == 0
    n = t // FFN_ROWS
    xspec = pl.BlockSpec((FFN_ROWS, d), lambda i: (i, 0))
    c_in, c_out, c_shapes = _side_cast(cast, n, lambda i: i)
    wspec = lambda shape: _const_spec(shape, single_buffer=True)
    return pl.pallas_call(
        functools.partial(_ffn_kernel, final=final, n_cast=len(cast)),
        grid=(n,),
        in_specs=[xspec, _const_spec((1, d)), wspec((d, d_ff)), wspec((d, d_ff)),
                  wspec((d_ff, d)), _const_spec((1, d))] + c_in,
        out_specs=[xspec] + c_out,
        out_shape=[jax.ShapeDtypeStruct((t, d), F32)] + c_shapes,
        input_output_aliases={0: 0},
        compiler_params=_params(("arbitrary",)),
        name="dense_ffn",
    )(x, g.reshape(1, d), wg, wu, wd, g_final.reshape(1, d), *cast)


def _router_kernel(x_ref, g_ref, rw_ref, rb_ref, before_ref, cnt0_ref, sel_ref, gate_ref, cnt_ref,
                   carry):
    i = pl.program_id(0)

    @pl.when(i == 0)
    def _():
        carry[...] = cnt0_ref[...]

    h = _rmsnorm(x_ref[...], g_ref[...])
    h_hi = h.astype(BF16)
    h_lo = (h - h_hi.astype(F32)).astype(BF16)
    by_hi = _dot(h_hi, rw_ref[...])
    logits = by_hi[:, :LANES] + (by_hi[:, LANES:] + _dot(h_lo, rw_ref[:, :LANES])) + rb_ref[...]
    lane_i = lax.broadcasted_iota(jnp.int32, logits.shape, 1)
    lane = lane_i.astype(F32)
    m0 = jnp.max(logits, axis=-1, keepdims=True)
    e0 = jnp.min(jnp.where(logits == m0, lane, float(LANES)), axis=-1, keepdims=True)
    rest = jnp.where(lane == e0, -jnp.inf, logits)
    m1 = jnp.max(rest, axis=-1, keepdims=True)
    e1 = jnp.min(jnp.where(rest == m1, lane, float(LANES)), axis=-1, keepdims=True)
    t = jnp.exp(m1 - m0)
    g0 = 1.0 / (1.0 + t)
    g1 = t * g0
    hot0 = (lane == e0).astype(F32)
    hot1 = (lane == e1).astype(F32)
    both = hot0 + hot1
    prior = _dot(before_ref[...], both.astype(BF16)) + carry[...]
    rank0 = jnp.sum(hot0 * prior, axis=-1, keepdims=True)
    rank1 = jnp.sum(hot1 * prior, axis=-1, keepdims=True)
    carry[...] += jnp.sum(both, axis=0, keepdims=True)
    cnt_ref[...] = carry[...]
    sel = jnp.where(lane_i == 0, e0, jnp.where(lane_i == 1, e1,
                    jnp.where(lane_i == 2, rank0, jnp.where(lane_i == 3, rank1, 0.0))))
    sel_ref[...] = jnp.transpose(sel)[:SUBLANES]
    gate_ref[...] = jnp.where(lane_i == 0, g0, jnp.where(lane_i == 1, g1, 0.0))


def _router(x, g, rw_parts, rb_pad, counts_before):
    t, d = x.shape
    rows = FFN_ROWS
    before = jnp.tril(jnp.ones((rows, rows), F32), -1).astype(BF16)
    return pl.pallas_call(
        _router_kernel,
        grid=(t // rows,),
        in_specs=[pl.BlockSpec((rows, d), lambda i: (i, 0)), _const_spec((1, d)),
                  _const_spec((d, 2 * LANES)), _const_spec((1, LANES)), _const_spec((rows, rows)),
                  _const_spec((1, LANES))],
        out_specs=[pl.BlockSpec((SUBLANES, rows), lambda i: (0, i)),
                   pl.BlockSpec((rows, LANES), lambda i: (i, 0)), _const_spec((1, LANES))],
        out_shape=[jax.ShapeDtypeStruct((SUBLANES, t), F32),
                   jax.ShapeDtypeStruct((t, LANES), F32),
                   jax.ShapeDtypeStruct((1, LANES), F32)],
        scratch_shapes=[pltpu.VMEM((1, LANES), F32)],
        compiler_params=_params(("arbitrary",)),
        name="moe_router",
    )(x, g.reshape(1, d), rw_parts, rb_pad, before, counts_before)


def _dispatch_kernel(meta_ref, xa_ref, xb_ref, p0_ref, p1_ref, xs_ref, zero_sc, sem, zsem, *,
                     n_a, n_exp, min_tiles):
    i = pl.program_id(0)
    rows = xa_ref.shape[0] * SUBLANES
    tile = zero_sc.shape[0]
    n_tiles = xs_ref.shape[0] // tile

    def zero_copy(row_end):
        start = pl.multiple_of(row_end - tile, tile)
        return pltpu.make_async_copy(zero_sc, xs_ref.at[pl.ds(start, tile)], zsem)

    fills = [(meta_ref[n_exp + e] > 0, meta_ref[e]) for e in range(n_exp)]
    fills += [(k >= meta_ref[2 * n_exp], (k + 1) * tile) for k in range(min_tiles, n_tiles)]

    @pl.when(i == 0)
    def _():
        zero_sc[...] = jnp.zeros_like(zero_sc)
        for needed, row_end in fills:
            @pl.when(needed)
            def _():
                zero_copy(row_end).start()
        for needed, row_end in fills:
            @pl.when(needed)
            def _():
                zero_copy(row_end).wait()

    def scatter_rows(x_ref):
        def start(grp, c):
            base = pl.multiple_of(grp * SUBLANES, SUBLANES)
            for u in range(SUBLANES):
                src = x_ref.at[grp, pl.ds(u, 1)]
                pltpu.make_async_copy(src, xs_ref.at[pl.ds(p0_ref[base + u], 1)], sem).start()
                pltpu.make_async_copy(src, xs_ref.at[pl.ds(p1_ref[base + u], 1)], sem).start()
            return c

        lax.fori_loop(0, rows // SUBLANES, start, 0)
        for _ in range(TOP_K * rows // tile):
            pltpu.make_async_copy(zero_sc, xs_ref.at[pl.ds(0, tile)], sem).wait()

    @pl.when(i < n_a)
    def _():
        scatter_rows(xa_ref)

    @pl.when(i >= n_a)
    def _():
        scatter_rows(xb_ref)


def _dispatch(xa, xb, pos0, pos1, meta, n_exp, n_sorted, tile):
    d = xa.shape[1]
    rows = DISPATCH_ROWS
    n_a, n_b = xa.shape[0] // rows, xb.shape[0] // rows
    t = xa.shape[0] + xb.shape[0]
    smem_rows = pl.BlockSpec((rows,), lambda i, *_: (i,), memory_space=pltpu.SMEM)
    min_tiles = -(-TOP_K * t // tile)
    assert rows % tile == 0
    grp = rows // SUBLANES
    grouped = lambda x: x.reshape(x.shape[0] // SUBLANES, SUBLANES, d)
    x_specs = [pl.BlockSpec((grp, SUBLANES, d), lambda i, *_: (jnp.minimum(i, n_a - 1), 0, 0)),
               pl.BlockSpec((grp, SUBLANES, d), lambda i, *_: (jnp.maximum(i - n_a, 0), 0, 0))]
    return pl.pallas_call(
        functools.partial(_dispatch_kernel, n_a=n_a, n_exp=n_exp, min_tiles=min_tiles),
        grid_spec=pltpu.PrefetchScalarGridSpec(
            num_scalar_prefetch=1,
            grid=(n_a + n_b,),
            in_specs=x_specs + [smem_rows, smem_rows],
            out_specs=pl.BlockSpec(memory_space=pl.ANY),
            scratch_shapes=[pltpu.VMEM((tile, d), F32), pltpu.SemaphoreType.DMA,
                            pltpu.SemaphoreType.DMA]),
        out_shape=jax.ShapeDtypeStruct((n_sorted, d), F32),
        compiler_params=_params(("arbitrary",)),
        name="moe_dispatch",
    )(meta, grouped(xa), grouped(xb), pos0, pos1)


def _expert_kernel(te_ref, nu_ref, xs_ref, g_ref, wg_ref, wu_ref, wd_ref, ys_ref):
    used = pl.program_id(0) < nu_ref[0]

    @pl.when(used)
    def _():
        hb = _rmsnorm(xs_ref[...], g_ref[...]).astype(BF16)
        _swiglu_into(ys_ref, hb, wg_ref, wu_ref, wd_ref, None)

    @pl.when(jnp.logical_not(used))
    def _():
        ys_ref[...] = jnp.zeros_like(ys_ref)


def _expert_ffn(xs, g, tile_expert, n_used, wg, wu, wd, tile):
    n_sorted, d = xs.shape
    d_ff = wg.shape[-1]
    rowmap = lambda i, te, nu: (jnp.minimum(i, nu[0] - 1), 0)
    wmap = lambda i, te, nu: (te[i], 0, 0)
    return pl.pallas_call(
        _expert_kernel,
        grid_spec=pltpu.PrefetchScalarGridSpec(
            num_scalar_prefetch=2,
            grid=(n_sorted // tile,),
            in_specs=[pl.BlockSpec((tile, d), rowmap),
                      pl.BlockSpec((1, d), lambda i, te, nu: (0, 0)),
                      pl.BlockSpec((None, d, d_ff), wmap),
                      pl.BlockSpec((None, d, d_ff), wmap),
                      pl.BlockSpec((None, d_ff, d), wmap)],
            out_specs=pl.BlockSpec((tile, d), lambda i, te, nu: (i, 0))),
        out_shape=jax.ShapeDtypeStruct((n_sorted, d), F32),
        compiler_params=_params(("arbitrary",)),
        name="expert_ffn",
    )(tile_expert, n_used, xs, g.reshape(1, d), wg, wu, wd)


def _combine_kernel(x_ref, p0_ref, p1_ref, p0n_ref, p1n_ref, gate_ref, gf_ref, ys_ref, o_ref, buf,
                    sem, *, final):
    i = pl.program_id(0)
    rows, d = x_ref.shape
    slot = i % 2

    def gather(pa_ref, pb_ref, to_slot):
        def start(grp, c):
            base = pl.multiple_of(grp * SUBLANES, SUBLANES)
            for u in range(SUBLANES):
                for k, p_ref in enumerate((pa_ref, pb_ref)):
                    pltpu.make_async_copy(ys_ref.at[pl.ds(p_ref[base + u], 1)],
                                          buf.at[to_slot, k, grp, pl.ds(u, 1)],
                                          sem.at[to_slot]).start()
            return c

        lax.fori_loop(0, rows // SUBLANES, start, 0)

    @pl.when(i == 0)
    def _():
        gather(p0_ref, p1_ref, 0)

    @pl.when(i + 1 < pl.num_programs(0))
    def _():
        gather(p0n_ref, p1n_ref, 1 - slot)

    for k in range(TOP_K):
        pltpu.make_async_copy(ys_ref.at[pl.ds(0, rows)], o_ref, sem.at[slot]).wait()
    gates = gate_ref[...]
    y0, y1 = (buf[slot, k].reshape(rows, d) for k in range(TOP_K))
    out = x_ref[...] + gates[:, 0:1] * y0 + gates[:, 1:2] * y1
    if final:
        out = _rmsnorm(out, gf_ref[...])
    o_ref[...] = out


def _combine(x, off_rows, pos0, pos1, gates, ys, g_final, final):
    n_rows, d = x.shape
    rows = COMBINE_ROWS
    off = off_rows // rows
    n = n_rows // rows
    smem_rows = pl.BlockSpec((rows,), lambda i: (off + i,), memory_space=pltpu.SMEM)
    smem_next = pl.BlockSpec((rows,), lambda i: (off + jnp.minimum(i + 1, n - 1),),
                             memory_space=pltpu.SMEM)
    xspec = pl.BlockSpec((rows, d), lambda i: (i, 0))
    return pl.pallas_call(
        functools.partial(_combine_kernel, final=final),
        grid=(n,),
        in_specs=[xspec, smem_rows, smem_rows, smem_next, smem_next,
                  pl.BlockSpec((rows, LANES), lambda i: (i, 0)), _const_spec((1, d)),
                  pl.BlockSpec(memory_space=pl.ANY)],
        out_specs=xspec,
        out_shape=jax.ShapeDtypeStruct((n_rows, d), F32),
        scratch_shapes=[pltpu.VMEM((2, TOP_K, rows // SUBLANES, SUBLANES, d), F32),
                        pltpu.SemaphoreType.DMA((2,))],
        compiler_params=_params(("arbitrary",)),
        name="moe_combine",
    )(x, pos0, pos1, pos0, pos1, gates, g_final.reshape(1, d), ys)


def _moe_layer(xa, xb, g, rw, rb, wg, wu, wd, g_final, final):
    d = xa.shape[1]
    t = xa.shape[0] + xb.shape[0]
    n_exp = rw.shape[-1]
    tile = FFN_ROWS
    rw_pad = jnp.zeros((d, LANES), F32).at[:, :n_exp].set(rw)
    rw_hi = rw_pad.astype(BF16)
    rw_parts = jnp.concatenate([rw_hi, (rw_pad - rw_hi.astype(F32)).astype(BF16)], axis=1)
    rb_pad = jnp.full((1, LANES), -jnp.inf, F32).at[0, :n_exp].set(rb)
    sel_a, gates_a, counts_a = _router(xa, g, rw_parts, rb_pad, jnp.zeros((1, LANES), F32))
    sel_b, gates_b, counts = _router(xb, g, rw_parts, rb_pad, counts_a)
    sel = jnp.concatenate([sel_a, sel_b], axis=1)

    counts = counts[0, :n_exp].astype(jnp.int32)
    gpad = (counts + tile - 1) // tile * tile
    gend = jnp.cumsum(gpad)
    gstart = gend - gpad
    sel = sel.astype(jnp.int32)
    pos0 = gstart[sel[0]] + sel[2]
    pos1 = gstart[sel[1]] + sel[3]
    n_tiles = (TOP_K * t + n_exp * (tile - 1)) // tile
    tile_ids = jnp.arange(n_tiles, dtype=jnp.int32)
    tile_expert = jnp.minimum(
        jnp.sum((gend // tile)[None, :] <= tile_ids[:, None], axis=1), n_exp - 1).astype(jnp.int32)
    n_used = (gend[-1:] // tile).astype(jnp.int32)

    meta = jnp.concatenate([gend, gpad, n_used]).astype(jnp.int32)
    xs = _dispatch(xa, xb, pos0, pos1, meta, n_exp, n_tiles * tile, tile)
    ys = _expert_ffn(xs, g, tile_expert, n_used, wg, wu, wd, tile)
    return (_combine(xa, 0, pos0, pos1, gates_a, ys, g_final, final),
            _combine(xb, xa.shape[0], pos0, pos1, gates_b, ys, g_final, final))


def kernel(x_prompt, x_sample, cache_mem_k, cache_mem_v, state_pool, mem_prompt, norm_mix_g, w_in, pool_w, pool_scale, gm_ln_g, gm_ln_b, gm_ws, gm_bs, w_mix_out, norm_xa_g, norm_mem_g, xa_wq, xa_wk, xa_wv, xa_wo, norm_ffn_g, ffn_wg, ffn_wu, ffn_wd, moe_router_w, moe_router_b, moe_wg, moe_wu, moe_wd, norm_final_g):
    batch, seq, d = x_prompt.shape
    dec_batch, dec_seq, _ = x_sample.shape
    depth = norm_mix_g.shape[0]
    n_mem, n_mem_heads, mem_hd = cache_mem_k.shape[2:]
    d_pool = pool_scale.shape[-1]
    d_gate = gm_ln_g.shape[-1]
    rows_p, rows_s = batch * seq, dec_batch * dec_seq
    seg_per_tile = GATE_ROWS // dec_seq
    assert seq % MIX_ROWS == 0 and MIX_ROWS % (MIX_SUB * GATE_ROWS) == 0
    assert GATE_ROWS % GMLP_CHUNK == 0
    assert GATE_ROWS % dec_seq == 0 and dec_batch % seg_per_tile == 0 and dec_seq >= HIST_ROWS
    assert dec_seq <= GMLP_CHUNK and PAST_LEN % GMLP_CHUNK == 0
    assert dec_batch % ATTN_SAMPLE_STREAMS == 0
    assert rows_p % FFN_ROWS == 0 and rows_s % FFN_ROWS == 0
    assert rows_p % DISPATCH_ROWS == 0 and rows_s % DISPATCH_ROWS == 0
    assert rows_p % COMBINE_ROWS == 0 and rows_s % COMBINE_ROWS == 0

    w = dict(norm_mix_g=norm_mix_g, w_in=w_in.astype(BF16), pool_w=pool_w.astype(BF16),
             pool_scale=pool_scale, gm_ln_g=gm_ln_g, gm_ln_b=gm_ln_b,
             w_mix_out=w_mix_out.astype(BF16), norm_xa_g=norm_xa_g,
             xa_wq=xa_wq.astype(BF16), xa_wo=xa_wo.astype(BF16))

    mk, mv, new_mem_k, new_mem_v = _memory_kv(mem_prompt, norm_mem_g, xa_wk, xa_wv, n_mem_heads)
    pk, pv = mk.reshape(depth, batch, n_mem, d), mv.reshape(depth, batch, n_mem, d)

    hist_p0 = jnp.zeros((batch, HIST_ROWS, d_pool), F32)
    hist_s = jnp.pad(state_pool, ((0, 0), (0, 0), (HIST_ROWS - state_pool.shape[2], 0), (0, 0)))
    keep = HIST_ROWS - state_pool.shape[2]

    xp = x_prompt.reshape(rows_p, d)
    xs = x_sample.reshape(rows_s, d)
    pool_p, pool_s, v_rows = [], [], []
    n_exp, _, d_ff = moe_wg.shape[1:]
    expert_w = {}
    for l in range(depth):
        wbig_p, bias_p = _gate_operands(gm_ws[l], gm_bs[l], GATE_ROWS, GMLP_CHUNK)
        wbig_s, bias_s = _gate_operands(gm_ws[l], gm_bs[l], GATE_ROWS, dec_seq)
        if l % 2 == 1:
            mix_cast = [moe_wd[l // 2].reshape(n_exp * d_ff, d)]
        else:
            mix_cast = [ffn_wg[l // 2], ffn_wu[l // 2], ffn_wd[l // 2]]
        xp, hp, *mix_done = _mixer(xp, l > 0, batch, seq // MIX_ROWS, 1, MIX_ROWS // MIX_SUB, 0,
                                   hist_p0, l, w, wbig_p, bias_p, False, mix_cast,
                                   (pk, pv, n_mem_heads), MIX_SUB)
        if l % 2 == 1:
            expert_w["wd"] = mix_done[0].reshape(n_exp, d_ff, d)
        else:
            ffn_b = mix_done
        xs, hs, vr = _mixer(xs, l > 0, dec_batch // seg_per_tile, 1, seg_per_tile, dec_seq,
                            PAST_LEN, hist_s[l], l, w, wbig_s, bias_s, True)
        pool_p.append(hp[:, keep:])
        pool_s.append(hs[:, keep:])
        v_rows.append(vr.reshape(dec_batch, dec_seq, d_gate))

        xs = _attention_cached(xs, ATTN_SAMPLE_STREAMS, cache_mem_k, cache_mem_v, l, w)

        final = l == depth - 1
        if l % 2 == 0:
            ffn_cast = ([moe_wg[l // 2].reshape(n_exp * d, d_ff),
                         moe_wu[l // 2].reshape(n_exp * d, d_ff)] if l + 1 < depth else [])
            xp, *ffn_done = _dense_ffn(xp, norm_ffn_g[l], *ffn_b, norm_final_g, final, ffn_cast)
            xs, = _dense_ffn(xs, norm_ffn_g[l], *ffn_b, norm_final_g, final)
            if ffn_done:
                expert_w["wg"], expert_w["wu"] = (a.reshape(n_exp, d, d_ff) for a in ffn_done)
        else:
            i = l // 2
            xp, xs = _moe_layer(xp, xs, norm_ffn_g[l], moe_router_w[i], moe_router_b[i],
                                expert_w["wg"], expert_w["wu"], expert_w["wd"], norm_final_g, final)

    return (xp.reshape(batch, seq, d), xs.reshape(dec_batch, dec_seq, d),
            new_mem_k, new_mem_v, jnp.stack(pool_p), jnp.stack(pool_s), jnp.stack(v_rows))
```
